```python
import jax, jax.numpy as jnp
from jax import lax
import numpy as np

D_MODEL = 1024
BATCH = 2
SEQ = 16384
DEPTH = 4

P_DIM = 256
N_MIXERS = 2
CONV_WIDTH = 3
M_HEADS = 4
QK_HEAD = D_MODEL // 2 // M_HEADS
V_HEAD = D_MODEL // M_HEADS
QK_DIM = M_HEADS * QK_HEAD
V_DIM = M_HEADS * V_HEAD
MLSTM_IN = 2 * QK_DIM + 2 * V_DIM + 2 * M_HEADS
CHUNK = 64
D_FF = 2816
EPS = 1e-6
N_CONV_LAYERS = (DEPTH + 1) // 2
N_MLSTM_LAYERS = DEPTH // 2

kernel_name = "hybrid_shortconv_mlstm_convglu_sandwich"


def rms_norm(x, g):
    xf = x.astype(jnp.float32)
    y = xf * lax.rsqrt(jnp.mean(xf * xf, axis=-1, keepdims=True) + EPS)
    return (y * g.astype(jnp.float32)).astype(x.dtype)


def causal_dwconv(x, w):
    k = w.shape[0]
    return lax.conv_general_dilated(
        x, w.astype(x.dtype)[:, None, :], window_strides=(1,), padding=((k - 1, 0),),
        dimension_numbers=("NWC", "WIO", "NWC"), feature_group_count=x.shape[-1])


def short_conv_mixer(x, w_in, w_conv, w_out):
    b_gate, c_gate, u = jnp.split(x @ w_in, 3, axis=-1)
    return (b_gate * causal_dwconv(c_gate * u, w_conv)) @ w_out


def mlstm_chunk_step(carry, xs):
    c_state, n_state, m_state = carry
    q, k, v, ig, lf = xs
    b = jnp.cumsum(lf, axis=-1)
    causal = jnp.tril(jnp.ones((CHUNK, CHUNK), dtype=bool))
    log_d = jnp.where(causal, b[..., :, None] - b[..., None, :] + ig[..., None, :], -jnp.inf)
    log_inter = b + m_state[..., None]
    m_row = jnp.maximum(log_inter, jnp.max(log_d, axis=-1))
    w_intra = jnp.exp(log_d - m_row[..., None])
    w_inter = jnp.exp(log_inter - m_row)
    s = jnp.einsum("bhld,bhsd->bhls", q, k) * w_intra
    num = w_inter[..., None] * jnp.einsum("bhld,bhde->bhle", q, c_state) + jnp.einsum("bhls,bhse->bhle", s, v)
    den = w_inter * jnp.einsum("bhld,bhd->bhl", q, n_state) + jnp.sum(s, axis=-1)
    h = num / jnp.maximum(jnp.abs(den), jnp.exp(-m_row))[..., None]
    b_last = b[..., -1]
    log_a = b_last[..., None] - b + ig
    m_new = jnp.maximum(b_last + m_state, jnp.max(log_a, axis=-1))
    decay = jnp.exp(b_last + m_state - m_new)
    w_a = jnp.exp(log_a - m_new[..., None])
    c_new = decay[..., None, None] * c_state + jnp.einsum("bhs,bhsd,bhse->bhde", w_a, k, v)
    n_new = decay[..., None] * n_state + jnp.einsum("bhs,bhsd->bhd", w_a, k)
    return (c_new, n_new, m_new), h


def mlstm_mixer(x, w_in, gate_bias, head_norm, w_out):
    bsz, seq, _ = x.shape
    nc = seq // CHUNK
    q, k, v, o, ig, fg = jnp.split(
        x @ w_in, [QK_DIM, 2 * QK_DIM, 2 * QK_DIM + V_DIM, 2 * QK_DIM + 2 * V_DIM,
                   2 * QK_DIM + 2 * V_DIM + M_HEADS], axis=-1)
    gb = gate_bias.astype(jnp.float32)
    ig = ig.astype(jnp.float32) + gb[:M_HEADS]
    lf = jax.nn.log_sigmoid(fg.astype(jnp.float32) + gb[M_HEADS:])

    def to_chunks(t, dh):
        return t.astype(jnp.float32).reshape(bsz, nc, CHUNK, M_HEADS, dh).transpose(1, 0, 3, 2, 4)

    def gate_chunks(t):
        return t.reshape(bsz, nc, CHUNK, M_HEADS).transpose(1, 0, 3, 2)

    qc = to_chunks(q, QK_HEAD)
    kc = to_chunks(k, QK_HEAD) * (QK_HEAD ** -0.5)
    vc = to_chunks(v, V_HEAD)
    init = (jnp.zeros((bsz, M_HEADS, QK_HEAD, V_HEAD), jnp.float32),
            jnp.zeros((bsz, M_HEADS, QK_HEAD), jnp.float32),
            jnp.zeros((bsz, M_HEADS), jnp.float32))
    _, h = lax.scan(mlstm_chunk_step, init, (qc, kc, vc, gate_chunks(ig), gate_chunks(lf)))
    h = h.transpose(1, 0, 3, 2, 4).reshape(bsz, seq, M_HEADS, V_HEAD)
    h = h * lax.rsqrt(jnp.mean(h * h, axis=-1, keepdims=True) + EPS)
    h = (h.reshape(bsz, seq, V_DIM) * head_norm.astype(jnp.float32)).astype(x.dtype)
    return (h * jax.nn.sigmoid(o)) @ w_out


def conv_glu(x, w_up, w_conv, w_down):
    g, u = jnp.split(x @ w_up, 2, axis=-1)
    return (jax.nn.gelu(causal_dwconv(g, w_conv), approximate=True) * u) @ w_down


def setup_inputs(seed: int = 0) -> dict:
    key = jax.random.key(seed)
    ks = jax.random.split(key, 24)

    def nrm(k, shape, scale):
        return jax.random.normal(k, shape, jnp.float32) * scale

    def gain(k, shape):
        return 1.0 + 0.05 * jax.random.normal(k, shape, jnp.float32)

    forget_bias = jnp.linspace(3.0, 6.0, M_HEADS, dtype=jnp.float32)
    gate_bias = jnp.concatenate([
        nrm(ks[12], (N_MLSTM_LAYERS, M_HEADS), 0.1),
        forget_bias[None, :] + nrm(ks[13], (N_MLSTM_LAYERS, M_HEADS), 0.1)], axis=-1)
    return {
        "x": nrm(ks[0], (BATCH, SEQ, D_MODEL), 1.0),
        "p": nrm(ks[1], (DEPTH, BATCH, SEQ, P_DIM), 1.0),
        "norm_mix_pre": gain(ks[2], (DEPTH, D_MODEL)),
        "norm_mix_post": gain(ks[3], (DEPTH, D_MODEL)),
        "norm_ffn_pre": gain(ks[4], (DEPTH, D_MODEL)),
        "norm_ffn_post": gain(ks[5], (DEPTH, D_MODEL)),
        "sc_w_in": nrm(ks[6], (N_CONV_LAYERS, D_MODEL, 3 * D_MODEL), D_MODEL ** -0.5),
        "sc_w_conv": nrm(ks[7], (N_CONV_LAYERS, CONV_WIDTH, D_MODEL), CONV_WIDTH ** -0.5),
        "sc_w_out": nrm(ks[8], (N_CONV_LAYERS, D_MODEL, D_MODEL), D_MODEL ** -0.5),
        "ml_w_in": nrm(ks[9], (N_MLSTM_LAYERS, D_MODEL, MLSTM_IN), D_MODEL ** -0.5),
        "ml_gate_bias": gate_bias,
        "ml_head_norm": gain(ks[10], (N_MLSTM_LAYERS, V_DIM)),
        "ml_w_out": nrm(ks[11], (N_MLSTM_LAYERS, V_DIM, D_MODEL), V_DIM ** -0.5),
        "ffn_w_up": nrm(ks[14], (DEPTH, D_MODEL, 2 * D_FF), D_MODEL ** -0.5),
        "ffn_w_conv": nrm(ks[15], (DEPTH, CONV_WIDTH, D_FF), CONV_WIDTH ** -0.5),
        "ffn_w_down": nrm(ks[16], (DEPTH, D_FF, D_MODEL), D_FF ** -0.5),
        "ple_norm_gate": gain(ks[17], (DEPTH, D_MODEL)),
        "ple_w_gate": nrm(ks[18], (DEPTH, D_MODEL, D_MODEL), D_MODEL ** -0.5),
        "ple_w_proj": nrm(ks[19], (DEPTH, P_DIM, D_MODEL), P_DIM ** -0.5),
        "ple_norm_out": gain(ks[20], (DEPTH, D_MODEL)),
    }


def reference(x, p, norm_mix_pre, norm_mix_post, norm_ffn_pre, norm_ffn_post,
              sc_w_in, sc_w_conv, sc_w_out, ml_w_in, ml_gate_bias, ml_head_norm, ml_w_out,
              ffn_w_up, ffn_w_conv, ffn_w_down, ple_norm_gate, ple_w_gate, ple_w_proj, ple_norm_out):
    for i in range(DEPTH):
        j = i // N_MIXERS
        h = rms_norm(x, norm_mix_pre[i])
        if i % N_MIXERS == 0:
            y = short_conv_mixer(h, sc_w_in[j], sc_w_conv[j], sc_w_out[j])
        else:
            y = mlstm_mixer(h, ml_w_in[j], ml_gate_bias[j], ml_head_norm[j], ml_w_out[j])
        x = x + rms_norm(y, norm_mix_post[i])
        y = conv_glu(rms_norm(x, norm_ffn_pre[i]), ffn_w_up[i], ffn_w_conv[i], ffn_w_down[i])
        x = x + rms_norm(y, norm_ffn_post[i])
        gate = jax.nn.sigmoid(rms_norm(x, ple_norm_gate[i]) @ ple_w_gate[i])
        x = x + rms_norm(gate * (p[i] @ ple_w_proj[i]), ple_norm_out[i])
    return x
```

```python
import functools

import jax
import jax.numpy as jnp
from jax import lax
from jax.experimental import pallas as pl
from jax.experimental.pallas import tpu as pltpu

EPS = 1e-6
N_MIXERS = 2
M_HEADS = 4
CONV_WIDTH = 3
SUBLANES = 8
LANES = 128
TM = 512
MLSTM_CHUNK = 256
VMEM_LIMIT_BYTES = 56 * 1024 * 1024


def _rms(x, g):
    return x * lax.rsqrt(jnp.mean(x * x, axis=-1, keepdims=True) + EPS) * g


def _causal_conv3(cur, tail_ref, w, first):
    t = cur.shape[0]
    prev = jnp.where(first, 0.0, tail_ref[...])
    tail_ref[...] = cur[t - SUBLANES:, :]
    row = lax.broadcasted_iota(jnp.int32, (SUBLANES, cur.shape[1]), 0)
    r1 = pltpu.roll(cur, 1, 0)
    r2 = pltpu.roll(cur, 2, 0)
    h1 = jnp.where(row == 0, prev[7:8, :], r1[:SUBLANES])
    h2 = jnp.where(row == 0, prev[6:7, :], jnp.where(row == 1, prev[7:8, :], r2[:SUBLANES]))
    r1 = jnp.concatenate([h1, r1[SUBLANES:]], axis=0)
    r2 = jnp.concatenate([h2, r2[SUBLANES:]], axis=0)
    return w[0:1, :] * r2 + w[1:2, :] * r1 + w[2:3, :] * cur


def _dot(a, b):
    return jnp.dot(a, b, preferred_element_type=jnp.float32)


def _first_of_row(tiles_per_row):
    return pl.program_id(0) % tiles_per_row == 0


def _conv_mixer_body(tiles_per_row, x_ref, gpre_ref, win_ref, wconv_ref, wout_ref, gpost_ref,
                     o_ref, tail_ref):
    d = x_ref.shape[1]
    x = x_ref[...]
    h = _rms(x, gpre_ref[...]).astype(jnp.bfloat16)
    z = _dot(h, win_ref[...])
    cu = z[:, d:2 * d] * z[:, 2 * d:]
    conv = _causal_conv3(cu, tail_ref, wconv_ref[...], _first_of_row(tiles_per_row))
    y = _dot((z[:, :d] * conv).astype(jnp.bfloat16), wout_ref[...])
    o_ref[...] = x + _rms(y, gpost_ref[...])


def _log_sigmoid(x):
    return jnp.minimum(x, 0.0) - jnp.log1p(jnp.exp(-jnp.abs(x)))


def _cumsum_rows(x):
    n = x.shape[0]
    row = lax.broadcasted_iota(jnp.int32, x.shape, 0)
    s = 1
    while s < n:
        x = x + jnp.where(row >= s, pltpu.roll(x, s, 0), 0.0)
        s *= 2
    return x


def _cumsum_lanes(x):
    n = x.shape[1]
    col = lax.broadcasted_iota(jnp.int32, x.shape, 1)
    s = 1
    while s < n:
        x = x + jnp.where(col >= s, pltpu.roll(x, s, 1), 0.0)
        s *= 2
    return x


def _mlstm_body(tiles_per_row, x_ref, gpre_ref, win_ref, wg_ref, wgt_ref, brow_ref, bcol_ref,
                hnorm_ref, wout_ref, gpost_ref, o_ref, c_ref, n_ref, m_ref, hbuf_ref):
    d = x_ref.shape[1]
    tm = x_ref.shape[0]
    qk = d // 2
    dk = qk // M_HEADS
    dv = d // M_HEADS
    L = MLSTM_CHUNK

    @pl.when(_first_of_row(tiles_per_row))
    def _():
        c_ref[...] = jnp.zeros_like(c_ref)
        n_ref[...] = jnp.zeros_like(n_ref)
        m_ref[...] = jnp.zeros_like(m_ref)

    x = x_ref[...]
    h = _rms(x, gpre_ref[...]).astype(jnp.bfloat16)
    z = _dot(h, win_ref[...])
    gc = _dot(h, wg_ref[...]) + brow_ref[...]
    gr = lax.dot_general(wgt_ref[...], h, (((1,), (1,)), ((), ())),
                         preferred_element_type=jnp.float32) + bcol_ref[:, 0:1]

    rows_ge = (lax.broadcasted_iota(jnp.int32, (L, L), 0)
               >= lax.broadcasted_iota(jnp.int32, (L, L), 1))
    scale = dk ** -0.5

    for c in range(tm // L):
        r0 = c * L
        gcc = gc[r0:r0 + L, :]
        ig_c = gcc
        b_c = _cumsum_rows(_log_sigmoid(gcc))
        grc = gr[:, r0:r0 + L]
        ig_r = grc[0:M_HEADS, :]
        b_r = _cumsum_lanes(_log_sigmoid(grc[M_HEADS:, :]))
        for j in range(M_HEADS):
            q = z[r0:r0 + L, j * dk:(j + 1) * dk]
            k = z[r0:r0 + L, qk + j * dk: qk + (j + 1) * dk] * scale
            v = z[r0:r0 + L, 2 * qk + j * dv: 2 * qk + (j + 1) * dv].astype(jnp.bfloat16)
            o = z[r0:r0 + L, 2 * qk + d + j * dv: 2 * qk + d + (j + 1) * dv]
            qb = q.astype(jnp.bfloat16)
            bcol = b_c[:, M_HEADS + j: M_HEADS + j + 1]
            icol = ig_c[:, j:j + 1]
            brow = b_r[j:j + 1, :]
            irow = ig_r[j:j + 1, :]
            c_st = c_ref[j]
            n_st = n_ref[j][0:1, :]
            m_st = m_ref[j][0:1, 0:1]

            log_d = jnp.where(rows_ge, bcol + (irow - brow), -jnp.inf)
            log_inter = bcol + m_st
            m_row = jnp.maximum(log_inter, jnp.max(log_d, axis=-1, keepdims=True))
            w_intra = jnp.exp(log_d - m_row)
            w_inter = jnp.exp(log_inter - m_row)
            s = lax.dot_general(qb, k.astype(jnp.bfloat16), (((1,), (1,)), ((), ())),
                                preferred_element_type=jnp.float32) * w_intra
            num = w_inter * _dot(qb, c_st.astype(jnp.bfloat16)) + _dot(s.astype(jnp.bfloat16), v)
            den = (w_inter * jnp.sum(q * n_st, axis=-1, keepdims=True)
                   + jnp.sum(s, axis=-1, keepdims=True))
            hh = num / jnp.maximum(jnp.abs(den), jnp.exp(-m_row))
            hh = hh * lax.rsqrt(jnp.mean(hh * hh, axis=-1, keepdims=True) + EPS)
            hh = hh * hnorm_ref[:, j * dv:(j + 1) * dv]
            hbuf_ref[r0:r0 + L, j * dv:(j + 1) * dv] = (hh * jax.nn.sigmoid(o)).astype(jnp.bfloat16)

            b_last = bcol[L - 1:L, :]
            log_a = b_last - bcol + icol
            m_new = jnp.maximum(b_last + m_st, jnp.max(log_a, axis=0, keepdims=True))
            decay = jnp.exp(b_last + m_st - m_new)
            kw = k * jnp.exp(log_a - m_new)
            c_ref[j] = decay * c_st + lax.dot_general(
                kw.astype(jnp.bfloat16), v, (((0,), (0,)), ((), ())),
                preferred_element_type=jnp.float32)
            n_ref[j] = jnp.broadcast_to(decay * n_st + jnp.sum(kw, axis=0, keepdims=True),
                                        n_ref.shape[1:])
            m_ref[j] = jnp.broadcast_to(m_new, m_ref.shape[1:])

    y = _dot(hbuf_ref[...], wout_ref[...])
    o_ref[...] = x + _rms(y, gpost_ref[...])


def _gelu_tanh(x):
    return 0.5 * x * (1.0 + jnp.tanh(0.7978845608028654 * (x + 0.044715 * (x * x * x))))


def _ffn_ple_body(tiles_per_row, x_ref, p_ref, gfpre_ref, wup_ref, wconv_ref, wdown_ref, gfpost_ref,
                  gple_ref, wgate_ref, wproj_ref, gout_ref, o_ref, tail_ref):
    f = wdown_ref.shape[0]
    x = x_ref[...]
    h = _rms(x, gfpre_ref[...]).astype(jnp.bfloat16)
    gu = _dot(h, wup_ref[...])
    conv = _causal_conv3(gu[:, :f], tail_ref, wconv_ref[...], _first_of_row(tiles_per_row))
    act = (_gelu_tanh(conv) * gu[:, f:]).astype(jnp.bfloat16)
    x = x + _rms(_dot(act, wdown_ref[...]), gfpost_ref[...])
    gate = jax.nn.sigmoid(_dot(_rms(x, gple_ref[...]).astype(jnp.bfloat16), wgate_ref[...]))
    pp = _dot(p_ref[...].astype(jnp.bfloat16), wproj_ref[...])
    o_ref[...] = x + _rms(gate * pp, gout_ref[...])


def _tile_spec(cols):
    return pl.BlockSpec((TM, cols), lambda i: (i, 0))


def _whole_spec(shape):
    return pl.BlockSpec(shape, lambda i: (0,) * len(shape), pipeline_mode=pl.Buffered(1))


_PARAMS = pltpu.CompilerParams(dimension_semantics=("arbitrary",), vmem_limit_bytes=VMEM_LIMIT_BYTES)


def _conv_mixer(x, seq, g_pre, w_in, w_conv, w_out, g_post):
    n, d = x.shape
    return pl.pallas_call(
        functools.partial(_conv_mixer_body, seq // TM),
        grid=(n // TM,),
        in_specs=[_tile_spec(d), _whole_spec(g_pre.shape), _whole_spec(w_in.shape),
                  _whole_spec(w_conv.shape), _whole_spec(w_out.shape), _whole_spec(g_post.shape)],
        out_specs=_tile_spec(d),
        out_shape=jax.ShapeDtypeStruct(x.shape, x.dtype),
        scratch_shapes=[pltpu.VMEM((SUBLANES, d), jnp.float32)],
        compiler_params=_PARAMS,
        name="conv_mixer",
    )(x, g_pre, w_in, w_conv, w_out, g_post)


def _mlstm_mixer(x, seq, g_pre, w_in, w_g, w_gt, b_row, b_col, h_norm, w_out, g_post):
    n, d = x.shape
    dk = d // 2 // M_HEADS
    dv = d // M_HEADS
    args = (x, g_pre, w_in, w_g, w_gt, b_row, b_col, h_norm, w_out, g_post)
    return pl.pallas_call(
        functools.partial(_mlstm_body, seq // TM),
        grid=(n // TM,),
        in_specs=[_tile_spec(d)] + [_whole_spec(a.shape) for a in args[1:]],
        out_specs=_tile_spec(d),
        out_shape=jax.ShapeDtypeStruct(x.shape, x.dtype),
        scratch_shapes=[pltpu.VMEM((M_HEADS, dk, dv), jnp.float32),
                        pltpu.VMEM((M_HEADS, SUBLANES, dk), jnp.float32),
                        pltpu.VMEM((M_HEADS, SUBLANES, LANES), jnp.float32),
                        pltpu.VMEM((TM, d), jnp.bfloat16)],
        compiler_params=_PARAMS,
        name="mlstm_mixer",
    )(*args)


def _ffn_ple(x, p, seq, g_fpre, w_up, w_conv, w_down, g_fpost, g_ple, w_gate, w_proj, g_out):
    n, d = x.shape
    args = (x, p, g_fpre, w_up, w_conv, w_down, g_fpost, g_ple, w_gate, w_proj, g_out)
    return pl.pallas_call(
        functools.partial(_ffn_ple_body, seq // TM),
        grid=(n // TM,),
        in_specs=[_tile_spec(d), _tile_spec(p.shape[1])] + [_whole_spec(a.shape) for a in args[2:]],
        out_specs=_tile_spec(d),
        out_shape=jax.ShapeDtypeStruct(x.shape, x.dtype),
        scratch_shapes=[pltpu.VMEM((SUBLANES, w_down.shape[0]), jnp.float32)],
        compiler_params=_PARAMS,
        name="ffn_ple",
    )(*args)


def kernel(x, p, norm_mix_pre, norm_mix_post, norm_ffn_pre, norm_ffn_post, sc_w_in, sc_w_conv, sc_w_out, ml_w_in, ml_gate_bias, ml_head_norm, ml_w_out, ffn_w_up, ffn_w_conv, ffn_w_down, ple_norm_gate, ple_w_gate, ple_w_proj, ple_norm_out):
    bsz, seq, d = x.shape
    depth = p.shape[0]
    assert seq % TM == 0 and TM % MLSTM_CHUNK == 0
    bf = jnp.bfloat16
    qkvo = 3 * d
    n_gates = 2 * M_HEADS

    xf = x.reshape(bsz * seq, d)
    pf = p.reshape(depth, bsz * seq, p.shape[-1])
    for i in range(depth):
        j = i // N_MIXERS
        if i % N_MIXERS == 0:
            xf = _conv_mixer(xf, seq, norm_mix_pre[i][None], sc_w_in[j].astype(bf), sc_w_conv[j],
                             sc_w_out[j].astype(bf), norm_mix_post[i][None])
        else:
            w_g = ml_w_in[j][:, qkvo:]
            w_g_pad = jnp.pad(w_g, ((0, 0), (0, LANES - n_gates))).astype(bf)
            bias = ml_gate_bias[j].astype(jnp.float32)
            b_row = jnp.pad(bias, (0, LANES - n_gates))[None]
            b_col = jnp.broadcast_to(bias[:, None], (n_gates, LANES))
            xf = _mlstm_mixer(xf, seq, norm_mix_pre[i][None], ml_w_in[j][:, :qkvo].astype(bf),
                              w_g_pad, w_g.T.astype(bf), b_row, b_col, ml_head_norm[j][None],
                              ml_w_out[j].astype(bf), norm_mix_post[i][None])
        xf = _ffn_ple(xf, pf[i], seq, norm_ffn_pre[i][None], ffn_w_up[i].astype(bf), ffn_w_conv[i],
                      ffn_w_down[i].astype(bf), norm_ffn_post[i][None], ple_norm_gate[i][None],
                      ple_w_gate[i].astype(bf), ple_w_proj[i].astype(bf), ple_norm_out[i][None])
    return xf.reshape(bsz, seq, d)
```

```python
import functools

import jax
import jax.numpy as jnp
from jax import lax
from jax.experimental import pallas as pl
from jax.experimental.pallas import tpu as pltpu

EPS = 1e-6
N_MIXERS = 2
M_HEADS = 4
CONV_WIDTH = 3
SUBLANES = 8
LANES = 128
TM = 512
MLSTM_CHUNK = 256
VMEM_LIMIT_BYTES = 56 * 1024 * 1024


def _rms(x, g):
    return x * lax.rsqrt(jnp.mean(x * x, axis=-1, keepdims=True) + EPS) * g


def _sigmoid(x):
    return 0.5 * jnp.tanh(0.5 * x) + 0.5


def _causal_conv3(cur, tail_ref, w, first):
    t = cur.shape[0]
    prev = jnp.where(first, 0.0, tail_ref[...])
    tail_ref[...] = cur[t - SUBLANES:, :]
    back1 = prev[SUBLANES - 1:SUBLANES, :]
    back2 = prev[SUBLANES - 2:SUBLANES - 1, :]
    row = lax.broadcasted_iota(jnp.int32, (SUBLANES, cur.shape[1]), 0)
    r1 = pltpu.roll(cur, 1, 0)
    r2 = pltpu.roll(cur, 2, 0)
    h1 = jnp.where(row == 0, back1, r1[:SUBLANES])
    h2 = jnp.where(row == 0, back2, jnp.where(row == 1, back1, r2[:SUBLANES]))
    r1 = jnp.concatenate([h1, r1[SUBLANES:]], axis=0)
    r2 = jnp.concatenate([h2, r2[SUBLANES:]], axis=0)
    return w[0:1, :] * r2 + w[1:2, :] * r1 + w[2:3, :] * cur


def _dot(a, b):
    return jnp.dot(a, b, preferred_element_type=jnp.float32)


def _first_of_row(tiles_per_row):
    return pl.program_id(0) % tiles_per_row == 0


def _conv_mixer_body(tiles_per_row, x_ref, gpre_ref, win_ref, wconv_ref, wout_ref, gpost_ref,
                     o_ref, tail_ref):
    d = x_ref.shape[1]
    x = x_ref[...]
    h = _rms(x, gpre_ref[...]).astype(jnp.bfloat16)
    z = _dot(h, win_ref[...])
    cu = z[:, d:2 * d] * z[:, 2 * d:]
    conv = _causal_conv3(cu, tail_ref, wconv_ref[...], _first_of_row(tiles_per_row))
    y = _dot((z[:, :d] * conv).astype(jnp.bfloat16), wout_ref[...])
    o_ref[...] = x + _rms(y, gpost_ref[...])


def _log_sigmoid(x):
    return jnp.minimum(x, 0.0) - jnp.log1p(jnp.exp(-jnp.abs(x)))


def _cumsum_rows(x):
    n = x.shape[0]
    row = lax.broadcasted_iota(jnp.int32, x.shape, 0)
    s = 1
    while s < n:
        x = x + jnp.where(row >= s, pltpu.roll(x, s, 0), 0.0)
        s *= 2
    return x


def _cumsum_lanes(x):
    n = x.shape[1]
    col = lax.broadcasted_iota(jnp.int32, x.shape, 1)
    s = 1
    while s < n:
        x = x + jnp.where(col >= s, pltpu.roll(x, s, 1), 0.0)
        s *= 2
    return x


def _mlstm_body(tiles_per_row, x_ref, gpre_ref, win_ref, wg_ref, wgt_ref, brow_ref, bcol_ref,
                hnorm_ref, wout_ref, gpost_ref, o_ref, c_ref, n_ref, m_ref, hbuf_ref):
    d = x_ref.shape[1]
    tm = x_ref.shape[0]
    qk = d // 2
    dk = qk // M_HEADS
    dv = d // M_HEADS
    L = MLSTM_CHUNK

    @pl.when(_first_of_row(tiles_per_row))
    def _():
        c_ref[...] = jnp.zeros_like(c_ref)
        n_ref[...] = jnp.zeros_like(n_ref)
        m_ref[...] = jnp.zeros_like(m_ref)

    x = x_ref[...]
    h = _rms(x, gpre_ref[...]).astype(jnp.bfloat16)
    z = _dot(h, win_ref[...])
    gc = _dot(h, wg_ref[...]) + brow_ref[...]
    gr = lax.dot_general(wgt_ref[...], h, (((1,), (1,)), ((), ())),
                         preferred_element_type=jnp.float32) + bcol_ref[:, 0:1]

    rows_ge = (lax.broadcasted_iota(jnp.int32, (L, L), 0)
               >= lax.broadcasted_iota(jnp.int32, (L, L), 1))
    scale = dk ** -0.5

    for c in range(tm // L):
        r0 = c * L
        gcc = gc[r0:r0 + L, :]
        ig_c = gcc
        b_c = _cumsum_rows(_log_sigmoid(gcc))
        grc = gr[:, r0:r0 + L]
        ig_r = grc[0:M_HEADS, :]
        b_r = _cumsum_lanes(_log_sigmoid(grc[M_HEADS:, :]))
        for j in range(M_HEADS):
            q = z[r0:r0 + L, j * dk:(j + 1) * dk]
            k = z[r0:r0 + L, qk + j * dk: qk + (j + 1) * dk] * scale
            v = z[r0:r0 + L, 2 * qk + j * dv: 2 * qk + (j + 1) * dv].astype(jnp.bfloat16)
            o = z[r0:r0 + L, 2 * qk + d + j * dv: 2 * qk + d + (j + 1) * dv]
            qb = q.astype(jnp.bfloat16)
            bcol = b_c[:, M_HEADS + j: M_HEADS + j + 1]
            icol = ig_c[:, j:j + 1]
            brow = b_r[j:j + 1, :]
            irow = ig_r[j:j + 1, :]
            c_st = c_ref[j]
            n_st = n_ref[j][0:1, :]
            m_st = m_ref[j][0:1, 0:1]

            log_d = jnp.where(rows_ge, bcol + (irow - brow), -jnp.inf)
            m_loc = jnp.max(log_d, axis=-1, keepdims=True)
            s = lax.dot_general(qb, k.astype(jnp.bfloat16), (((1,), (1,)), ((), ())),
                                preferred_element_type=jnp.float32) * jnp.exp(log_d - m_loc)
            sv = _dot(s.astype(jnp.bfloat16), v)
            s_sum = jnp.sum(s, axis=-1, keepdims=True)
            b_last = bcol[L - 1:L, :]
            log_a = b_last - bcol + icol
            m_a = jnp.max(log_a, axis=0, keepdims=True)
            kw = k * jnp.exp(log_a - m_a)
            u = lax.dot_general(kw.astype(jnp.bfloat16), v, (((0,), (0,)), ((), ())),
                                preferred_element_type=jnp.float32)
            kw_sum = jnp.sum(kw, axis=0, keepdims=True)

            log_inter = bcol + m_st
            m_row = jnp.maximum(log_inter, m_loc)
            w_inter = jnp.exp(log_inter - m_row)
            w_loc = jnp.exp(m_loc - m_row)
            num = w_inter * _dot(qb, c_st.astype(jnp.bfloat16)) + w_loc * sv
            den = w_inter * jnp.sum(q * n_st, axis=-1, keepdims=True) + w_loc * s_sum
            hh = num * (1.0 / jnp.maximum(jnp.abs(den), jnp.exp(-m_row)))
            hh = hh * lax.rsqrt(jnp.mean(hh * hh, axis=-1, keepdims=True) + EPS)
            hh = hh * hnorm_ref[:, j * dv:(j + 1) * dv]
            hbuf_ref[r0:r0 + L, j * dv:(j + 1) * dv] = (hh * _sigmoid(o)).astype(jnp.bfloat16)

            m_new = jnp.maximum(b_last + m_st, m_a)
            keep = jnp.exp(b_last + m_st - m_new)
            add = jnp.exp(m_a - m_new)
            c_ref[j] = keep * c_st + add * u
            n_ref[j] = jnp.broadcast_to(keep * n_st + add * kw_sum, n_ref.shape[1:])
            m_ref[j] = jnp.broadcast_to(m_new, m_ref.shape[1:])

    y = _dot(hbuf_ref[...], wout_ref[...])
    o_ref[...] = x + _rms(y, gpost_ref[...])


def _gelu_tanh(x):
    return 0.5 * x * (1.0 + jnp.tanh(0.7978845608028654 * (x + 0.044715 * (x * x * x))))


def _ffn_ple_body(tiles_per_row, x_ref, p_ref, gfpre_ref, wup_ref, wconv_ref, wdown_ref, gfpost_ref,
                  gple_ref, wgate_ref, wproj_ref, gout_ref, o_ref, tail_ref):
    f = wdown_ref.shape[0]
    x = x_ref[...]
    h = _rms(x, gfpre_ref[...]).astype(jnp.bfloat16)
    gu = _dot(h, wup_ref[...])
    conv = _causal_conv3(gu[:, :f], tail_ref, wconv_ref[...], _first_of_row(tiles_per_row))
    act = (_gelu_tanh(conv) * gu[:, f:]).astype(jnp.bfloat16)
    x = x + _rms(_dot(act, wdown_ref[...]), gfpost_ref[...])
    gate = _sigmoid(_dot(_rms(x, gple_ref[...]).astype(jnp.bfloat16), wgate_ref[...]))
    pp = _dot(p_ref[...].astype(jnp.bfloat16), wproj_ref[...])
    o_ref[...] = x + _rms(gate * pp, gout_ref[...])


def _tile_spec(cols):
    return pl.BlockSpec((TM, cols), lambda i: (i, 0))


def _layer_spec(stacked, layer):
    return pl.BlockSpec((None,) + stacked.shape[1:], lambda i: (layer, 0, 0),
                        pipeline_mode=pl.Buffered(1))


_PARAMS = pltpu.CompilerParams(dimension_semantics=("arbitrary",), vmem_limit_bytes=VMEM_LIMIT_BYTES)


def _conv_mixer(x, seq, layer, mixer, g_pre, w_in, w_conv, w_out, g_post):
    n, d = x.shape
    return pl.pallas_call(
        functools.partial(_conv_mixer_body, seq // TM),
        grid=(n // TM,),
        in_specs=[_tile_spec(d), _layer_spec(g_pre, layer), _layer_spec(w_in, mixer),
                  _layer_spec(w_conv, mixer), _layer_spec(w_out, mixer), _layer_spec(g_post, layer)],
        out_specs=_tile_spec(d),
        out_shape=jax.ShapeDtypeStruct(x.shape, x.dtype),
        scratch_shapes=[pltpu.VMEM((SUBLANES, d), jnp.float32)],
        compiler_params=_PARAMS,
        name="conv_mixer",
    )(x, g_pre, w_in, w_conv, w_out, g_post)


def _mlstm_mixer(x, seq, layer, mixer, g_pre, w_in, w_g, w_gt, b_row, b_col, h_norm, w_out, g_post):
    n, d = x.shape
    dk = d // 2 // M_HEADS
    dv = d // M_HEADS
    per_mixer = (w_in, w_g, w_gt, b_row, b_col, h_norm, w_out)
    return pl.pallas_call(
        functools.partial(_mlstm_body, seq // TM),
        grid=(n // TM,),
        in_specs=([_tile_spec(d), _layer_spec(g_pre, layer)]
                  + [_layer_spec(a, mixer) for a in per_mixer] + [_layer_spec(g_post, layer)]),
        out_specs=_tile_spec(d),
        out_shape=jax.ShapeDtypeStruct(x.shape, x.dtype),
        scratch_shapes=[pltpu.VMEM((M_HEADS, dk, dv), jnp.float32),
                        pltpu.VMEM((M_HEADS, SUBLANES, dk), jnp.float32),
                        pltpu.VMEM((M_HEADS, SUBLANES, LANES), jnp.float32),
                        pltpu.VMEM((TM, d), jnp.bfloat16)],
        compiler_params=_PARAMS,
        name="mlstm_mixer",
    )(x, g_pre, *per_mixer, g_post)


def _ffn_ple(x, p, seq, layer, *per_layer):
    n, d = x.shape
    w_down = per_layer[3]
    return pl.pallas_call(
        functools.partial(_ffn_ple_body, seq // TM),
        grid=(n // TM,),
        in_specs=([_tile_spec(d), pl.BlockSpec((None, TM, p.shape[2]), lambda i: (layer, i, 0))]
                  + [_layer_spec(a, layer) for a in per_layer]),
        out_specs=_tile_spec(d),
        out_shape=jax.ShapeDtypeStruct(x.shape, x.dtype),
        scratch_shapes=[pltpu.VMEM((SUBLANES, w_down.shape[1]), jnp.float32)],
        compiler_params=_PARAMS,
        name="ffn_ple",
    )(x, p, *per_layer)


def kernel(x, p, norm_mix_pre, norm_mix_post, norm_ffn_pre, norm_ffn_post, sc_w_in, sc_w_conv, sc_w_out, ml_w_in, ml_gate_bias, ml_head_norm, ml_w_out, ffn_w_up, ffn_w_conv, ffn_w_down, ple_norm_gate, ple_w_gate, ple_w_proj, ple_norm_out):
    bsz, seq, d = x.shape
    depth = p.shape[0]
    assert seq % TM == 0 and TM % MLSTM_CHUNK == 0
    bf = jnp.bfloat16
    qkvo = 3 * d
    n_gates = 2 * M_HEADS

    def rows(g):
        return g[:, None, :]

    w_g = ml_w_in[:, :, qkvo:]
    w_g_pad = jnp.pad(w_g, ((0, 0), (0, 0), (0, LANES - n_gates))).astype(bf)
    w_gt = jnp.swapaxes(w_g, 1, 2).astype(bf)
    bias = ml_gate_bias.astype(jnp.float32)
    b_row = rows(jnp.pad(bias, ((0, 0), (0, LANES - n_gates))))
    b_col = jnp.broadcast_to(bias[:, :, None], bias.shape + (LANES,))
    mlstm_w = (ml_w_in[:, :, :qkvo].astype(bf), w_g_pad, w_gt, b_row, b_col, rows(ml_head_norm),
               ml_w_out.astype(bf))
    conv_w = (sc_w_in.astype(bf), sc_w_conv, sc_w_out.astype(bf))
    ffn_w = (rows(norm_ffn_pre), ffn_w_up.astype(bf), ffn_w_conv, ffn_w_down.astype(bf),
             rows(norm_ffn_post), rows(ple_norm_gate), ple_w_gate.astype(bf), ple_w_proj.astype(bf),
             rows(ple_norm_out))
    g_pre, g_post = rows(norm_mix_pre), rows(norm_mix_post)

    xf = x.reshape(bsz * seq, d)
    pf = p.reshape(depth, bsz * seq, p.shape[-1])
    for i in range(depth):
        j = i // N_MIXERS
        if i % N_MIXERS == 0:
            xf = _conv_mixer(xf, seq, i, j, g_pre, *conv_w, g_post)
        else:
            xf = _mlstm_mixer(xf, seq, i, j, g_pre, *mlstm_w, g_post)
        xf = _ffn_ple(xf, pf, seq, i, *ffn_w)
    return xf.reshape(bsz, seq, d)
```

```python
import functools

import jax
import jax.numpy as jnp
from jax import lax
from jax.experimental import pallas as pl
from jax.experimental.pallas import tpu as pltpu

EPS = 1e-6
N_MIXERS = 2
M_HEADS = 4
CONV_WIDTH = 3
SUBLANES = 8
LANES = 128
BF16_ROWS = 16
COL_BLOCK = 256
ROW_GROUPS = 2
TM = 512
MLSTM_CHUNK = 256
VMEM_LIMIT_BYTES = 56 * 1024 * 1024


def _rms(x, g):
    return x * lax.rsqrt(jnp.mean(x * x, axis=-1, keepdims=True) + EPS) * g


def _sigmoid(x):
    return 0.5 * jnp.tanh(0.5 * x) + 0.5


def _causal_conv3(cur, prev, w):
    back1 = prev[SUBLANES - 1:SUBLANES, :]
    back2 = prev[SUBLANES - 2:SUBLANES - 1, :]
    row = lax.broadcasted_iota(jnp.int32, (SUBLANES, cur.shape[1]), 0)
    r1 = pltpu.roll(cur, 1, 0)
    r2 = pltpu.roll(cur, 2, 0)
    h1 = jnp.where(row == 0, back1, r1[:SUBLANES])
    h2 = jnp.where(row == 0, back2, jnp.where(row == 1, back1, r2[:SUBLANES]))
    r1 = jnp.concatenate([h1, r1[SUBLANES:]], axis=0)
    r2 = jnp.concatenate([h2, r2[SUBLANES:]], axis=0)
    return w[0:1, :] * r2 + w[1:2, :] * r1 + w[2:3, :] * cur


def _conv_blocks(pre, tail_ref, wconv_ref, first):
    out = [[] for _ in pre]
    for j in range(len(pre[0])):
        cols = slice(j * COL_BLOCK, (j + 1) * COL_BLOCK)
        prev = jnp.where(first, 0.0, tail_ref[:, cols])
        for i, group in enumerate(pre):
            cur = group[j]
            out[i].append(_causal_conv3(cur, prev, wconv_ref[:, cols]))
            prev = cur[cur.shape[0] - SUBLANES:, :]
        tail_ref[:, cols] = prev
    return out


def _dot(a, b):
    return jnp.dot(a, b, preferred_element_type=jnp.float32)


def _first_of_row(tiles_per_row):
    return pl.program_id(0) % tiles_per_row == 0


def _conv_mixer_body(tiles_per_row, x_ref, gpre_ref, win_ref, wconv_ref, wout_ref, gpost_ref,
                     o_ref, tail_ref):
    tm, d = x_ref.shape
    rows = [slice(i * tm // ROW_GROUPS, (i + 1) * tm // ROW_GROUPS) for i in range(ROW_GROUPS)]
    nb = d // COL_BLOCK
    x = [x_ref[r, :] for r in rows]
    h = [_rms(xi, gpre_ref[...]).astype(jnp.bfloat16) for xi in x]
    z = [_dot(hi, win_ref[...]) for hi in h]
    cu = [[zi[:, 2 * j * COL_BLOCK:(2 * j + 1) * COL_BLOCK] * zi[:, (2 * j + 1) * COL_BLOCK:(2 * j + 2) * COL_BLOCK]
           for j in range(nb)] for zi in z]
    conv = _conv_blocks(cu, tail_ref, wconv_ref, _first_of_row(tiles_per_row))
    y = [_dot((zi[:, 2 * d:] * jnp.concatenate(ci, axis=1)).astype(jnp.bfloat16), wout_ref[...])
         for zi, ci in zip(z, conv)]
    for r, xi, yi in zip(rows, x, y):
        o_ref[r, :] = xi + _rms(yi, gpost_ref[...])


def _log_sigmoid(x):
    return jnp.minimum(x, 0.0) - jnp.log1p(jnp.exp(-jnp.abs(x)))


def _cumsum_rows(x):
    n = x.shape[0]
    row = lax.broadcasted_iota(jnp.int32, x.shape, 0)
    s = 1
    while s < n:
        x = x + jnp.where(row >= s, pltpu.roll(x, s, 0), 0.0)
        s *= 2
    return x


def _cumsum_lanes(x):
    n = x.shape[1]
    col = lax.broadcasted_iota(jnp.int32, x.shape, 1)
    s = 1
    while s < n:
        x = x + jnp.where(col >= s, pltpu.roll(x, s, 1), 0.0)
        s *= 2
    return x


def _mlstm_body(tiles_per_row, x_ref, gpre_ref, wqk_ref, wvot_ref, wg_ref, wgt_ref, brow_ref, bcol_ref,
                hnorm_ref, wout_ref, gpost_ref, o_ref, ct_ref, n_ref, m_ref):
    d = x_ref.shape[1]
    tm = x_ref.shape[0]
    qk = d // 2
    dk = qk // M_HEADS
    dv = d // M_HEADS
    L = MLSTM_CHUNK
    nt = (((1,), (1,)), ((), ()))

    @pl.when(_first_of_row(tiles_per_row))
    def _():
        ct_ref[...] = jnp.zeros_like(ct_ref)
        n_ref[...] = jnp.zeros_like(n_ref)
        m_ref[...] = jnp.zeros_like(m_ref)

    x = x_ref[...]
    h = _rms(x, gpre_ref[...]).astype(jnp.bfloat16)
    gc = _dot(h, wg_ref[...]) + brow_ref[...]
    gr = lax.dot_general(wgt_ref[...], h, nt,
                         preferred_element_type=jnp.float32) + bcol_ref[:, 0:1]
    zqk = _dot(h, wqk_ref[...])

    row_i = lax.broadcasted_iota(jnp.int32, (L, L), 0)
    col_i = lax.broadcasted_iota(jnp.int32, (L, L), 1)
    key_le_query = row_i <= col_i
    scale = dk ** -0.5
    chunks = range(tm // L)
    heads = range(M_HEADS)

    qb = [[zqk[c * L:(c + 1) * L, j * dk:(j + 1) * dk].astype(jnp.bfloat16) for j in heads]
          for c in chunks]
    kb = [[(zqk[c * L:(c + 1) * L, qk + j * dk: qk + (j + 1) * dk] * scale).astype(jnp.bfloat16)
           for j in heads] for c in chunks]
    kq = [[lax.dot_general(kb[c][j], qb[c][j], nt, preferred_element_type=jnp.float32)
           for j in heads] for c in chunks]
    zvt = lax.dot_general(wvot_ref[:d, :], h, nt, preferred_element_type=jnp.float32)

    acol, brow, b_last, m_a, w_a = {}, {}, {}, {}, {}
    for c in chunks:
        gcc = gc[c * L:(c + 1) * L, :]
        b_c = _cumsum_rows(_log_sigmoid(gcc))
        a_c = pltpu.roll(gcc, M_HEADS, 1) - b_c
        grc = gr[:, c * L:(c + 1) * L]
        b_r = _cumsum_lanes(_log_sigmoid(grc[M_HEADS:, :]))
        for j in heads:
            acol[c, j] = a_c[:, M_HEADS + j: M_HEADS + j + 1]
            brow[c, j] = b_r[j:j + 1, :]
            b_last[c, j] = brow[c, j][:, L - 1:L]
            log_a = b_last[c, j] - brow[c, j] + grc[j:j + 1, :]
            m_a[c, j] = jnp.max(log_a, axis=1, keepdims=True)
            w_a[c, j] = jnp.exp(log_a - m_a[c, j])

    m_loc, s_sum, stb = {}, {}, {}
    for c in chunks:
        for j in heads:
            log_dt = jnp.where(key_le_query, brow[c, j] + acol[c, j], -jnp.inf)
            m_loc[c, j] = jnp.max(log_dt, axis=0, keepdims=True)
            st = kq[c][j] * jnp.exp(log_dt - m_loc[c, j])
            s_sum[c, j] = jnp.sum(st, axis=0, keepdims=True)
            stb[c, j] = st.astype(jnp.bfloat16)

    def state_terms(c, j):
        vt = zvt[j * dv:(j + 1) * dv, c * L:(c + 1) * L]
        w_a_rows = jnp.broadcast_to(w_a[c, j], (BF16_ROWS, L)).astype(jnp.bfloat16)
        return (_dot((vt * w_a[c, j]).astype(jnp.bfloat16), kb[c][j]),
                _dot(w_a_rows, kb[c][j]))

    last = chunks[-1]
    svt, ut, kw_sum = {}, {}, {}
    for c in chunks:
        for j in heads:
            vt = zvt[j * dv:(j + 1) * dv, c * L:(c + 1) * L]
            svt[c, j] = _dot(vt.astype(jnp.bfloat16), stb[c, j])
            if c != last:
                ut[c, j], kw_sum[c, j] = state_terms(c, j)

    ct_in, n_in, m_in, decay = {}, {}, {}, {}
    for j in heads:
        ct_st, n_st, m_st = ct_ref[j], n_ref[j], m_ref[j][0:1, 0:1]
        for c in chunks:
            ct_in[c, j], n_in[c, j], m_in[c, j] = ct_st, n_st, m_st
            m_new = jnp.maximum(b_last[c, j] + m_st, m_a[c, j])
            decay[c, j] = (jnp.exp(b_last[c, j] + m_st - m_new), jnp.exp(m_a[c, j] - m_new))
            if c != last:
                keep, add = decay[c, j]
                ct_st = keep * ct_st + add * ut[c, j]
                n_st = keep * n_st + add * kw_sum[c, j]
            m_st = m_new
        m_ref[j] = jnp.broadcast_to(m_st, m_ref.shape[1:])

    qct = {k: lax.dot_general(ct_in[k].astype(jnp.bfloat16), qb[k[0]][k[1]], nt,
                              preferred_element_type=jnp.float32) for k in ct_in}
    qn = {k: lax.dot_general(n_in[k].astype(jnp.bfloat16), qb[k[0]][k[1]], nt,
                             preferred_element_type=jnp.float32)[0:1, :] for k in n_in}
    zot = lax.dot_general(wvot_ref[d:, :], h, nt, preferred_element_type=jnp.float32)
    for j in heads:
        keep, add = decay[last, j]
        u_last, kw_last = state_terms(last, j)
        ct_ref[j] = keep * ct_in[last, j] + add * u_last
        n_ref[j] = keep * n_in[last, j] + add * kw_last

    eye = (row_i == col_i).astype(jnp.bfloat16)
    for c in chunks:
        rows = slice(c * L, (c + 1) * L)
        gated = []
        for j in heads:
            ot = zot[j * dv:(j + 1) * dv, rows]
            log_inter = brow[c, j] + m_in[c, j]
            m_row = jnp.maximum(log_inter, m_loc[c, j])
            w_inter = jnp.exp(log_inter - m_row)
            w_loc = jnp.exp(m_loc[c, j] - m_row)
            den = w_inter * qn[c, j] + w_loc * s_sum[c, j]
            ht = ((w_inter * qct[c, j] + w_loc * svt[c, j])
                  * (1.0 / jnp.maximum(jnp.abs(den), jnp.exp(-m_row))))
            ht = ht * lax.rsqrt(jnp.mean(ht * ht, axis=0, keepdims=True) + EPS)
            gated.append((ht * hnorm_ref[j * dv:(j + 1) * dv, :] * _sigmoid(ot)).astype(jnp.bfloat16))
        hb = jnp.concatenate(
            [lax.dot_general(eye, g, nt, preferred_element_type=jnp.float32).astype(jnp.bfloat16)
             for g in gated], axis=1)
        o_ref[rows, :] = x[rows, :] + _rms(_dot(hb, wout_ref[...]), gpost_ref[...])


def _gelu_tanh(x):
    return 0.5 * x * (1.0 + jnp.tanh(0.7978845608028654 * (x + 0.044715 * (x * x * x))))


def _ffn_ple_body(tiles_per_row, x_ref, p_ref, gfpre_ref, wup_ref, wconv_ref, wdown_ref, gfpost_ref,
                  gple_ref, wgate_ref, wproj_ref, gout_ref, o_ref, tail_ref):
    tm = x_ref.shape[0]
    f = wdown_ref.shape[0]
    rows = [slice(i * tm // ROW_GROUPS, (i + 1) * tm // ROW_GROUPS) for i in range(ROW_GROUPS)]
    nb = f // COL_BLOCK
    x = [x_ref[r, :] for r in rows]
    pp = [_dot(p_ref[r, :].astype(jnp.bfloat16), wproj_ref[...]) for r in rows]
    h = [_rms(xi, gfpre_ref[...]).astype(jnp.bfloat16) for xi in x]
    gu = [_dot(hi, wup_ref[...]) for hi in h]
    g = [[gi[:, 2 * j * COL_BLOCK:(2 * j + 1) * COL_BLOCK] for j in range(nb)] for gi in gu]
    conv = _conv_blocks(g, tail_ref, wconv_ref, _first_of_row(tiles_per_row))
    act = [jnp.concatenate(
        [(_gelu_tanh(ci[j]) * gi[:, (2 * j + 1) * COL_BLOCK:(2 * j + 2) * COL_BLOCK]).astype(jnp.bfloat16)
         for j in range(nb)], axis=1) for ci, gi in zip(conv, gu)]
    y = [_dot(ai, wdown_ref[...]) for ai in act]
    x = [xi + _rms(yi, gfpost_ref[...]) for xi, yi in zip(x, y)]
    h = [_rms(xi, gple_ref[...]).astype(jnp.bfloat16) for xi in x]
    gate = [_sigmoid(_dot(hi, wgate_ref[...])) for hi in h]
    for r, xi, gi, pi in zip(rows, x, gate, pp):
        o_ref[r, :] = xi + _rms(gi * pi, gout_ref[...])


def _tile_spec(cols):
    return pl.BlockSpec((TM, cols), lambda i: (i, 0))


def _layer_spec(stacked, layer):
    return pl.BlockSpec((None,) + stacked.shape[1:], lambda i: (layer, 0, 0),
                        pipeline_mode=pl.Buffered(1))


_PARAMS = pltpu.CompilerParams(dimension_semantics=("arbitrary",), vmem_limit_bytes=VMEM_LIMIT_BYTES)


def _conv_mixer(x, seq, layer, mixer, g_pre, w_in, w_conv, w_out, g_post):
    n, d = x.shape
    return pl.pallas_call(
        functools.partial(_conv_mixer_body, seq // TM),
        grid=(n // TM,),
        in_specs=[_tile_spec(d), _layer_spec(g_pre, layer), _layer_spec(w_in, mixer),
                  _layer_spec(w_conv, mixer), _layer_spec(w_out, mixer), _layer_spec(g_post, layer)],
        out_specs=_tile_spec(d),
        out_shape=jax.ShapeDtypeStruct(x.shape, x.dtype),
        scratch_shapes=[pltpu.VMEM((SUBLANES, d), jnp.float32)],
        compiler_params=_PARAMS,
        name="conv_mixer",
    )(x, g_pre, w_in, w_conv, w_out, g_post)


def _mlstm_mixer(x, seq, layer, mixer, g_pre, per_mixer, g_post):
    n, d = x.shape
    dk = d // 2 // M_HEADS
    dv = d // M_HEADS
    return pl.pallas_call(
        functools.partial(_mlstm_body, seq // TM),
        grid=(n // TM,),
        in_specs=([_tile_spec(d), _layer_spec(g_pre, layer)]
                  + [_layer_spec(a, mixer) for a in per_mixer] + [_layer_spec(g_post, layer)]),
        out_specs=_tile_spec(d),
        out_shape=jax.ShapeDtypeStruct(x.shape, x.dtype),
        scratch_shapes=[pltpu.VMEM((M_HEADS, dv, dk), jnp.float32),
                        pltpu.VMEM((M_HEADS, BF16_ROWS, dk), jnp.float32),
                        pltpu.VMEM((M_HEADS, SUBLANES, LANES), jnp.float32)],
        compiler_params=_PARAMS,
        name="mlstm_mixer",
    )(x, g_pre, *per_mixer, g_post)


def _ffn_ple(x, p, seq, layer, *per_layer):
    n, d = x.shape
    w_down = per_layer[3]
    return pl.pallas_call(
        functools.partial(_ffn_ple_body, seq // TM),
        grid=(n // TM,),
        in_specs=([_tile_spec(d), pl.BlockSpec((None, TM, p.shape[2]), lambda i: (layer, i, 0))]
                  + [_layer_spec(a, layer) for a in per_layer]),
        out_specs=_tile_spec(d),
        out_shape=jax.ShapeDtypeStruct(x.shape, x.dtype),
        scratch_shapes=[pltpu.VMEM((SUBLANES, w_down.shape[1]), jnp.float32)],
        compiler_params=_PARAMS,
        name="ffn_ple",
    )(x, p, *per_layer)


def _pair_blocks(w):
    layers, rows, two_c = w.shape
    blocks = two_c // 2 // COL_BLOCK
    w = w.reshape(layers, rows, 2, blocks, COL_BLOCK)
    return jnp.swapaxes(w, 2, 3).reshape(layers, rows, two_c)


def kernel(x, p, norm_mix_pre, norm_mix_post, norm_ffn_pre, norm_ffn_post, sc_w_in, sc_w_conv, sc_w_out, ml_w_in, ml_gate_bias, ml_head_norm, ml_w_out, ffn_w_up, ffn_w_conv, ffn_w_down, ple_norm_gate, ple_w_gate, ple_w_proj, ple_norm_out):
    bsz, seq, d = x.shape
    depth = p.shape[0]
    assert seq % TM == 0 and TM % MLSTM_CHUNK == 0
    bf = jnp.bfloat16
    qkvo = 3 * d
    n_gates = 2 * M_HEADS

    def rows(g):
        return g[:, None, :]

    w_g = ml_w_in[:, :, qkvo:]
    w_g_pad = jnp.pad(w_g, ((0, 0), (0, 0), (0, LANES - n_gates))).astype(bf)
    w_gt = jnp.swapaxes(w_g, 1, 2).astype(bf)
    bias = ml_gate_bias.astype(jnp.float32)
    b_row = rows(jnp.pad(bias, ((0, 0), (0, LANES - n_gates))))
    b_col = jnp.broadcast_to(bias[:, :, None], bias.shape + (LANES,))
    w_vot = jnp.swapaxes(ml_w_in[:, :, d:qkvo], 1, 2).astype(bf)
    h_norm = jnp.broadcast_to(ml_head_norm[:, :, None], ml_head_norm.shape + (MLSTM_CHUNK,))
    mlstm_w = (ml_w_in[:, :, :d].astype(bf), w_vot, w_g_pad, w_gt, b_row, b_col, h_norm,
               ml_w_out.astype(bf))
    conv_w = (jnp.concatenate([_pair_blocks(sc_w_in[:, :, d:]), sc_w_in[:, :, :d]], axis=2).astype(bf),
              sc_w_conv, sc_w_out.astype(bf))
    ffn_w = (rows(norm_ffn_pre), _pair_blocks(ffn_w_up).astype(bf), ffn_w_conv, ffn_w_down.astype(bf),
             rows(norm_ffn_post), rows(ple_norm_gate), ple_w_gate.astype(bf), ple_w_proj.astype(bf),
             rows(ple_norm_out))
    g_pre, g_post = rows(norm_mix_pre), rows(norm_mix_post)

    xf = x.reshape(bsz * seq, d)
    pf = p.reshape(depth, bsz * seq, p.shape[-1])
    for i in range(depth):
        j = i // N_MIXERS
        if i % N_MIXERS == 0:
            xf = _conv_mixer(xf, seq, i, j, g_pre, *conv_w, g_post)
        else:
            xf = _mlstm_mixer(xf, seq, i, j, g_pre, mlstm_w, g_post)
        xf = _ffn_ple(xf, pf, seq, i, *ffn_w)
    return xf.reshape(bsz, seq, d)
```

```python
import functools

import jax
import jax.numpy as jnp
from jax import lax
from jax.experimental import pallas as pl
from jax.experimental.pallas import tpu as pltpu

EPS = 1e-6
N_MIXERS = 2
M_HEADS = 4
CONV_WIDTH = 3
SUBLANES = 8
LANES = 128
BF16_ROWS = 16
COL_BLOCK = 256
ROW_GROUPS = 2
TM = 512
MLSTM_CHUNK = 256
VMEM_LIMIT_BYTES = 56 * 1024 * 1024


def _rms(x, g):
    return x * lax.rsqrt(jnp.mean(x * x, axis=-1, keepdims=True) + EPS) * g


def _sigmoid(x):
    return 0.5 * jnp.tanh(0.5 * x) + 0.5


def _causal_conv3(cur, prev, w):
    back1 = prev[SUBLANES - 1:SUBLANES, :]
    back2 = prev[SUBLANES - 2:SUBLANES - 1, :]
    row = lax.broadcasted_iota(jnp.int32, (SUBLANES, cur.shape[1]), 0)
    r1 = pltpu.roll(cur, 1, 0)
    r2 = pltpu.roll(cur, 2, 0)
    h1 = jnp.where(row == 0, back1, r1[:SUBLANES])
    h2 = jnp.where(row == 0, back2, jnp.where(row == 1, back1, r2[:SUBLANES]))
    r1 = jnp.concatenate([h1, r1[SUBLANES:]], axis=0)
    r2 = jnp.concatenate([h2, r2[SUBLANES:]], axis=0)
    return w[0:1, :] * r2 + w[1:2, :] * r1 + w[2:3, :] * cur


def _conv_blocks(pre, tail_ref, wconv_ref, first):
    out = [[] for _ in pre]
    for j in range(len(pre[0])):
        cols = slice(j * COL_BLOCK, (j + 1) * COL_BLOCK)
        prev = jnp.where(first, 0.0, tail_ref[:, cols])
        for i, group in enumerate(pre):
            cur = group[j]
            out[i].append(_causal_conv3(cur, prev, wconv_ref[:, cols]))
            prev = cur[cur.shape[0] - SUBLANES:, :]
        tail_ref[:, cols] = prev
    return out


def _dot(a, b):
    return jnp.dot(a, b, preferred_element_type=jnp.float32)


def _first_of_row(tiles_per_row):
    return pl.program_id(0) % tiles_per_row == 0


def _conv_mixer_body(tiles_per_row, x_ref, gpre_ref, win_ref, wconv_ref, wout_ref, gpost_ref,
                     o_ref, tail_ref):
    tm, d = x_ref.shape
    rows = [slice(i * tm // ROW_GROUPS, (i + 1) * tm // ROW_GROUPS) for i in range(ROW_GROUPS)]
    blocks = [slice(j * COL_BLOCK, (j + 1) * COL_BLOCK) for j in range(d // COL_BLOCK)]
    x = [x_ref[r, :] for r in rows]
    h = [_rms(xi, gpre_ref[...]).astype(jnp.bfloat16) for xi in x]
    cu = [[_dot(hi, win_ref[:, d + c.start:d + c.stop]) * _dot(hi, win_ref[:, 2 * d + c.start:2 * d + c.stop])
           for c in blocks] for hi in h]
    b_gate = [_dot(hi, win_ref[:, :d]) for hi in h]
    conv = _conv_blocks(cu, tail_ref, wconv_ref, _first_of_row(tiles_per_row))
    y = [_dot((bi * jnp.concatenate(ci, axis=1)).astype(jnp.bfloat16), wout_ref[...])
         for bi, ci in zip(b_gate, conv)]
    for r, xi, yi in zip(rows, x, y):
        o_ref[r, :] = xi + _rms(yi, gpost_ref[...])


def _log_sigmoid(x):
    return jnp.minimum(x, 0.0) - jnp.log1p(jnp.exp(-jnp.abs(x)))


def _cumsum_rows(x):
    n = x.shape[0]
    row = lax.broadcasted_iota(jnp.int32, x.shape, 0)
    s = 1
    while s < n:
        x = x + jnp.where(row >= s, pltpu.roll(x, s, 0), 0.0)
        s *= 2
    return x


def _cumsum_lanes(x):
    n = x.shape[1]
    col = lax.broadcasted_iota(jnp.int32, x.shape, 1)
    s = 1
    while s < n:
        x = x + jnp.where(col >= s, pltpu.roll(x, s, 1), 0.0)
        s *= 2
    return x


def _mlstm_body(tiles_per_row, x_ref, gpre_ref, wqk_ref, wvo_ref, wg_ref, wgt_ref, brow_ref, bcol_ref,
                hnorm_ref, wout_ref, gpost_ref, o_ref, wvot_ref, ct_ref, n_ref, m_ref):
    d = x_ref.shape[1]
    tm = x_ref.shape[0]
    qk = d // 2
    dk = qk // M_HEADS
    dv = d // M_HEADS
    L = MLSTM_CHUNK
    nt = (((1,), (1,)), ((), ()))

    row_i = lax.broadcasted_iota(jnp.int32, (L, L), 0)
    col_i = lax.broadcasted_iota(jnp.int32, (L, L), 1)
    key_le_query = row_i <= col_i
    eye = (row_i == col_i).astype(jnp.bfloat16)

    def transposed(a):
        return lax.dot_general(eye, a, nt, preferred_element_type=jnp.float32).astype(jnp.bfloat16)

    @pl.when(pl.program_id(0) == 0)
    def _():
        for r in range(wvot_ref.shape[0] // L):
            for c in range(wvot_ref.shape[1] // L):
                wvot_ref[r * L:(r + 1) * L, c * L:(c + 1) * L] = transposed(
                    wvo_ref[c * L:(c + 1) * L, r * L:(r + 1) * L])

    @pl.when(_first_of_row(tiles_per_row))
    def _():
        ct_ref[...] = jnp.zeros_like(ct_ref)
        n_ref[...] = jnp.zeros_like(n_ref)
        m_ref[...] = jnp.zeros_like(m_ref)

    x = x_ref[...]
    h = _rms(x, gpre_ref[...]).astype(jnp.bfloat16)
    gc = _dot(h, wg_ref[...]) + brow_ref[...]
    gr = lax.dot_general(wgt_ref[...], h, nt,
                         preferred_element_type=jnp.float32) + bcol_ref[:, 0:1]
    zqk = _dot(h, wqk_ref[...])

    scale = dk ** -0.5
    chunks = range(tm // L)
    heads = range(M_HEADS)

    qb = [[zqk[c * L:(c + 1) * L, j * dk:(j + 1) * dk].astype(jnp.bfloat16) for j in heads]
          for c in chunks]
    kb = [[(zqk[c * L:(c + 1) * L, qk + j * dk: qk + (j + 1) * dk] * scale).astype(jnp.bfloat16)
           for j in heads] for c in chunks]
    kq = [[lax.dot_general(kb[c][j], qb[c][j], nt, preferred_element_type=jnp.float32)
           for j in heads] for c in chunks]
    zvt = lax.dot_general(wvot_ref[:d, :], h, nt, preferred_element_type=jnp.float32)

    acol, brow, b_last, m_a, w_a = {}, {}, {}, {}, {}
    for c in chunks:
        gcc = gc[c * L:(c + 1) * L, :]
        b_c = _cumsum_rows(_log_sigmoid(gcc))
        a_c = pltpu.roll(gcc, M_HEADS, 1) - b_c
        grc = gr[:, c * L:(c + 1) * L]
        b_r = _cumsum_lanes(_log_sigmoid(grc[M_HEADS:, :]))
        for j in heads:
            acol[c, j] = a_c[:, M_HEADS + j: M_HEADS + j + 1]
            brow[c, j] = b_r[j:j + 1, :]
            b_last[c, j] = brow[c, j][:, L - 1:L]
            log_a = b_last[c, j] - brow[c, j] + grc[j:j + 1, :]
            m_a[c, j] = jnp.max(log_a, axis=1, keepdims=True)
            w_a[c, j] = jnp.exp(log_a - m_a[c, j])

    m_loc, s_sum, stb = {}, {}, {}
    for c in chunks:
        for j in heads:
            log_dt = jnp.where(key_le_query, brow[c, j] + acol[c, j], -jnp.inf)
            m_loc[c, j] = jnp.max(log_dt, axis=0, keepdims=True)
            st = kq[c][j] * jnp.exp(log_dt - m_loc[c, j])
            s_sum[c, j] = jnp.sum(st, axis=0, keepdims=True)
            stb[c, j] = st.astype(jnp.bfloat16)

    def state_terms(c, j):
        vt = zvt[j * dv:(j + 1) * dv, c * L:(c + 1) * L]
        w_a_rows = jnp.broadcast_to(w_a[c, j], (BF16_ROWS, L)).astype(jnp.bfloat16)
        return (_dot((vt * w_a[c, j]).astype(jnp.bfloat16), kb[c][j]),
                _dot(w_a_rows, kb[c][j]))

    last = chunks[-1]
    svt, ut, kw_sum = {}, {}, {}
    for c in chunks:
        for j in heads:
            vt = zvt[j * dv:(j + 1) * dv, c * L:(c + 1) * L]
            svt[c, j] = _dot(vt.astype(jnp.bfloat16), stb[c, j])
            if c != last:
                ut[c, j], kw_sum[c, j] = state_terms(c, j)

    ct_in, n_in, m_in, decay = {}, {}, {}, {}
    for j in heads:
        ct_st, n_st, m_st = ct_ref[j], n_ref[j], m_ref[j][0:1, 0:1]
        for c in chunks:
            ct_in[c, j], n_in[c, j], m_in[c, j] = ct_st, n_st, m_st
            m_new = jnp.maximum(b_last[c, j] + m_st, m_a[c, j])
            decay[c, j] = (jnp.exp(b_last[c, j] + m_st - m_new), jnp.exp(m_a[c, j] - m_new))
            if c != last:
                keep, add = decay[c, j]
                ct_st = keep * ct_st + add * ut[c, j]
                n_st = keep * n_st + add * kw_sum[c, j]
            m_st = m_new
        m_ref[j] = jnp.broadcast_to(m_st, m_ref.shape[1:])

    qct = {k: lax.dot_general(ct_in[k].astype(jnp.bfloat16), qb[k[0]][k[1]], nt,
                              preferred_element_type=jnp.float32) for k in ct_in}
    qn = {k: lax.dot_general(n_in[k].astype(jnp.bfloat16), qb[k[0]][k[1]], nt,
                             preferred_element_type=jnp.float32)[0:1, :] for k in n_in}
    zot = lax.dot_general(wvot_ref[d:, :], h, nt, preferred_element_type=jnp.float32)
    for j in heads:
        keep, add = decay[last, j]
        u_last, kw_last = state_terms(last, j)
        ct_ref[j] = keep * ct_in[last, j] + add * u_last
        n_ref[j] = keep * n_in[last, j] + add * kw_last

    for c in chunks:
        rows = slice(c * L, (c + 1) * L)
        gated = []
        for j in heads:
            ot = zot[j * dv:(j + 1) * dv, rows]
            log_inter = brow[c, j] + m_in[c, j]
            m_row = jnp.maximum(log_inter, m_loc[c, j])
            w_inter = jnp.exp(log_inter - m_row)
            w_loc = jnp.exp(m_loc[c, j] - m_row)
            den = w_inter * qn[c, j] + w_loc * s_sum[c, j]
            ht = ((w_inter * qct[c, j] + w_loc * svt[c, j])
                  * (1.0 / jnp.maximum(jnp.abs(den), jnp.exp(-m_row))))
            ht = ht * lax.rsqrt(jnp.mean(ht * ht, axis=0, keepdims=True) + EPS)
            gated.append((ht * hnorm_ref[j * dv:(j + 1) * dv, :] * _sigmoid(ot)).astype(jnp.bfloat16))
        hb = jnp.concatenate([transposed(g) for g in gated], axis=1)
        o_ref[rows, :] = x[rows, :] + _rms(_dot(hb, wout_ref[...]), gpost_ref[...])


def _gelu_tanh(x):
    return 0.5 * x * (1.0 + jnp.tanh(0.7978845608028654 * (x + 0.044715 * (x * x * x))))


def _ffn_ple_body(tiles_per_row, x_ref, p_ref, gfpre_ref, wup_ref, wconv_ref, wdown_ref, gfpost_ref,
                  gple_ref, wgate_ref, wproj_ref, gout_ref, o_ref, tail_ref):
    tm = x_ref.shape[0]
    f = wdown_ref.shape[0]
    rows = [slice(i * tm // ROW_GROUPS, (i + 1) * tm // ROW_GROUPS) for i in range(ROW_GROUPS)]
    blocks = [slice(j * COL_BLOCK, (j + 1) * COL_BLOCK) for j in range(f // COL_BLOCK)]
    x = [x_ref[r, :] for r in rows]
    pp = [_dot(p_ref[r, :].astype(jnp.bfloat16), wproj_ref[...]) for r in rows]
    h = [_rms(xi, gfpre_ref[...]).astype(jnp.bfloat16) for xi in x]
    g, u = [], []
    for hi in h:
        gu = [(_dot(hi, wup_ref[:, c]), _dot(hi, wup_ref[:, f + c.start:f + c.stop])) for c in blocks]
        g.append([pair[0] for pair in gu])
        u.append([pair[1] for pair in gu])
    conv = _conv_blocks(g, tail_ref, wconv_ref, _first_of_row(tiles_per_row))
    act = [jnp.concatenate([(_gelu_tanh(cij) * uij).astype(jnp.bfloat16) for cij, uij in zip(ci, ui)], axis=1)
           for ci, ui in zip(conv, u)]
    y = [_dot(ai, wdown_ref[...]) for ai in act]
    x = [xi + _rms(yi, gfpost_ref[...]) for xi, yi in zip(x, y)]
    h = [_rms(xi, gple_ref[...]).astype(jnp.bfloat16) for xi in x]
    gate = [_sigmoid(_dot(hi, wgate_ref[...])) for hi in h]
    for r, xi, gi, pi in zip(rows, x, gate, pp):
        o_ref[r, :] = xi + _rms(gi * pi, gout_ref[...])


def _tile_spec(cols):
    return pl.BlockSpec((TM, cols), lambda i: (i, 0))


def _layer_spec(stacked, layer):
    return pl.BlockSpec((None,) + stacked.shape[1:], lambda i: (layer, 0, 0),
                        pipeline_mode=pl.Buffered(1))


_PARAMS = pltpu.CompilerParams(dimension_semantics=("arbitrary",), vmem_limit_bytes=VMEM_LIMIT_BYTES)


def _conv_mixer(x, seq, layer, mixer, g_pre, w_in, w_conv, w_out, g_post):
    n, d = x.shape
    return pl.pallas_call(
        functools.partial(_conv_mixer_body, seq // TM),
        grid=(n // TM,),
        in_specs=[_tile_spec(d), _layer_spec(g_pre, layer), _layer_spec(w_in, mixer),
                  _layer_spec(w_conv, mixer), _layer_spec(w_out, mixer), _layer_spec(g_post, layer)],
        out_specs=_tile_spec(d),
        out_shape=jax.ShapeDtypeStruct(x.shape, x.dtype),
        scratch_shapes=[pltpu.VMEM((SUBLANES, d), jnp.float32)],
        compiler_params=_PARAMS,
        name="conv_mixer",
    )(x, g_pre, w_in, w_conv, w_out, g_post)


def _mlstm_mixer(x, seq, layer, mixer, g_pre, per_mixer, g_post):
    n, d = x.shape
    dk = d // 2 // M_HEADS
    dv = d // M_HEADS
    return pl.pallas_call(
        functools.partial(_mlstm_body, seq // TM),
        grid=(n // TM,),
        in_specs=([_tile_spec(d), _layer_spec(g_pre, layer)]
                  + [_layer_spec(a, mixer) for a in per_mixer] + [_layer_spec(g_post, layer)]),
        out_specs=_tile_spec(d),
        out_shape=jax.ShapeDtypeStruct(x.shape, x.dtype),
        scratch_shapes=[pltpu.VMEM((2 * d, d), jnp.bfloat16),
                        pltpu.VMEM((M_HEADS, dv, dk), jnp.float32),
                        pltpu.VMEM((M_HEADS, BF16_ROWS, dk), jnp.float32),
                        pltpu.VMEM((M_HEADS, SUBLANES, LANES), jnp.float32)],
        compiler_params=_PARAMS,
        name="mlstm_mixer",
    )(x, g_pre, *per_mixer, g_post)


def _ffn_ple(x, p, seq, layer, *per_layer):
    n, d = x.shape
    w_down = per_layer[3]
    return pl.pallas_call(
        functools.partial(_ffn_ple_body, seq // TM),
        grid=(n // TM,),
        in_specs=([_tile_spec(d), pl.BlockSpec((None, TM, p.shape[2]), lambda i: (layer, i, 0))]
                  + [_layer_spec(a, layer) for a in per_layer]),
        out_specs=_tile_spec(d),
        out_shape=jax.ShapeDtypeStruct(x.shape, x.dtype),
        scratch_shapes=[pltpu.VMEM((SUBLANES, w_down.shape[1]), jnp.float32)],
        compiler_params=_PARAMS,
        name="ffn_ple",
    )(x, p, *per_layer)


def kernel(x, p, norm_mix_pre, norm_mix_post, norm_ffn_pre, norm_ffn_post, sc_w_in, sc_w_conv, sc_w_out, ml_w_in, ml_gate_bias, ml_head_norm, ml_w_out, ffn_w_up, ffn_w_conv, ffn_w_down, ple_norm_gate, ple_w_gate, ple_w_proj, ple_norm_out):
    bsz, seq, d = x.shape
    depth = p.shape[0]
    assert seq % TM == 0 and TM % MLSTM_CHUNK == 0
    bf = jnp.bfloat16
    qkvo = 3 * d
    n_gates = 2 * M_HEADS

    def rows(g):
        return g[:, None, :]

    w_g = ml_w_in[:, :, qkvo:]
    w_g_pad = jnp.pad(w_g, ((0, 0), (0, 0), (0, LANES - n_gates))).astype(bf)
    w_gt = jnp.swapaxes(w_g, 1, 2).astype(bf)
    bias = ml_gate_bias.astype(jnp.float32)
    b_row = rows(jnp.pad(bias, ((0, 0), (0, LANES - n_gates))))
    b_col = jnp.broadcast_to(bias[:, :, None], bias.shape + (LANES,))
    h_norm = jnp.broadcast_to(ml_head_norm[:, :, None], ml_head_norm.shape + (MLSTM_CHUNK,))
    mlstm_w = (ml_w_in[:, :, :d].astype(bf), ml_w_in[:, :, d:qkvo].astype(bf), w_g_pad, w_gt, b_row, b_col, h_norm,
               ml_w_out.astype(bf))
    conv_w = (sc_w_in.astype(bf), sc_w_conv, sc_w_out.astype(bf))
    ffn_w = (rows(norm_ffn_pre), ffn_w_up.astype(bf), ffn_w_conv, ffn_w_down.astype(bf),
             rows(norm_ffn_post), rows(ple_norm_gate), ple_w_gate.astype(bf), ple_w_proj.astype(bf),
             rows(ple_norm_out))
    g_pre, g_post = rows(norm_mix_pre), rows(norm_mix_post)

    xf = x.reshape(bsz * seq, d)
    pf = p.reshape(depth, bsz * seq, p.shape[-1])
    for i in range(depth):
        j = i // N_MIXERS
        if i % N_MIXERS == 0:
            xf = _conv_mixer(xf, seq, i, j, g_pre, *conv_w, g_post)
        else:
            xf = _mlstm_mixer(xf, seq, i, j, g_pre, mlstm_w, g_post)
        xf = _ffn_ple(xf, pf, seq, i, *ffn_w)
    return xf.reshape(bsz, seq, d)
```

```python
import functools

import jax
import jax.numpy as jnp
from jax import lax
from jax.experimental import pallas as pl
from jax.experimental.pallas import tpu as pltpu

EPS = 1e-6
N_MIXERS = 2
M_HEADS = 4
CONV_WIDTH = 3
SUBLANES = 8
LANES = 128
BF16_ROWS = 16
COL_BLOCK = 256
GROUP_ROWS = 256
TM_CONV = 1024
TM_MLSTM = 1024
TM_FFN = 1024
MLSTM_CHUNK = 256
VMEM_LIMIT_BYTES = 56 * 1024 * 1024


def _rms(x, g):
    return x * lax.rsqrt(jnp.mean(x * x, axis=-1, keepdims=True) + EPS) * g


def _sigmoid(x):
    return 0.5 * jnp.tanh(0.5 * x) + 0.5


def _causal_conv3(cur, prev, w):
    back1 = prev[SUBLANES - 1:SUBLANES, :]
    back2 = prev[SUBLANES - 2:SUBLANES - 1, :]
    row = lax.broadcasted_iota(jnp.int32, (SUBLANES, cur.shape[1]), 0)
    r1 = pltpu.roll(cur, 1, 0)
    r2 = pltpu.roll(cur, 2, 0)
    h1 = jnp.where(row == 0, back1, r1[:SUBLANES])
    h2 = jnp.where(row == 0, back2, jnp.where(row == 1, back1, r2[:SUBLANES]))
    r1 = jnp.concatenate([h1, r1[SUBLANES:]], axis=0)
    r2 = jnp.concatenate([h2, r2[SUBLANES:]], axis=0)
    return w[0:1, :] * r2 + w[1:2, :] * r1 + w[2:3, :] * cur


def _conv_blocks(pre, tail_ref, wconv_ref, first):
    out = [[] for _ in pre]
    for j in range(len(pre[0])):
        cols = slice(j * COL_BLOCK, (j + 1) * COL_BLOCK)
        prev = jnp.where(first, 0.0, tail_ref[:, cols])
        for i, group in enumerate(pre):
            cur = group[j]
            out[i].append(_causal_conv3(cur, prev, wconv_ref[:, cols]))
            prev = cur[cur.shape[0] - SUBLANES:, :]
        tail_ref[:, cols] = prev
    return out


def _dot(a, b):
    return jnp.dot(a, b, preferred_element_type=jnp.float32)


def _first_of_row(tiles_per_row):
    return pl.program_id(0) % tiles_per_row == 0


def _conv_mixer_body(tiles_per_row, x_ref, gpre_ref, win_ref, wconv_ref, wout_ref, gpost_ref,
                     o_ref, tail_ref):
    tm, d = x_ref.shape
    rows = [slice(r, r + GROUP_ROWS) for r in range(0, tm, GROUP_ROWS)]
    blocks = [slice(j * COL_BLOCK, (j + 1) * COL_BLOCK) for j in range(d // COL_BLOCK)]
    x = [x_ref[r, :] for r in rows]
    h = [_rms(xi, gpre_ref[...]).astype(jnp.bfloat16) for xi in x]
    cu = [[_dot(hi, win_ref[:, d + c.start:d + c.stop]) * _dot(hi, win_ref[:, 2 * d + c.start:2 * d + c.stop])
           for c in blocks] for hi in h]
    b_gate = [_dot(hi, win_ref[:, :d]) for hi in h]
    conv = _conv_blocks(cu, tail_ref, wconv_ref, _first_of_row(tiles_per_row))
    y = [_dot((bi * jnp.concatenate(ci, axis=1)).astype(jnp.bfloat16), wout_ref[...])
         for bi, ci in zip(b_gate, conv)]
    for r, xi, yi in zip(rows, x, y):
        o_ref[r, :] = xi + _rms(yi, gpost_ref[...])


def _log_sigmoid(x):
    return jnp.minimum(x, 0.0) - jnp.log1p(jnp.exp(-jnp.abs(x)))


def _cumsum_rows(x):
    n = x.shape[0]
    row = lax.broadcasted_iota(jnp.int32, x.shape, 0)
    s = 1
    while s < n:
        x = x + jnp.where(row >= s, pltpu.roll(x, s, 0), 0.0)
        s *= 2
    return x


def _cumsum_lanes(x):
    n = x.shape[1]
    col = lax.broadcasted_iota(jnp.int32, x.shape, 1)
    s = 1
    while s < n:
        x = x + jnp.where(col >= s, pltpu.roll(x, s, 1), 0.0)
        s *= 2
    return x


def _mlstm_body(tiles_per_row, x_ref, gpre_ref, wqk_ref, wvo_ref, wg_ref, wgt_ref, brow_ref, bcol_ref,
                hnorm_ref, wout_ref, gpost_ref, o_ref, wvot_ref, ct_ref, n_ref, m_ref):
    d = x_ref.shape[1]
    tm = x_ref.shape[0]
    qk = d // 2
    dk = qk // M_HEADS
    dv = d // M_HEADS
    L = MLSTM_CHUNK
    nt = (((1,), (1,)), ((), ()))

    row_i = lax.broadcasted_iota(jnp.int32, (L, L), 0)
    col_i = lax.broadcasted_iota(jnp.int32, (L, L), 1)
    key_le_query = row_i <= col_i
    eye = (row_i == col_i).astype(jnp.bfloat16)

    def transposed(a):
        return lax.dot_general(eye, a, nt, preferred_element_type=jnp.float32).astype(jnp.bfloat16)

    @pl.when(pl.program_id(0) == 0)
    def _():
        for r in range(wvot_ref.shape[0] // L):
            for c in range(wvot_ref.shape[1] // L):
                wvot_ref[r * L:(r + 1) * L, c * L:(c + 1) * L] = transposed(
                    wvo_ref[c * L:(c + 1) * L, r * L:(r + 1) * L])

    @pl.when(_first_of_row(tiles_per_row))
    def _():
        ct_ref[...] = jnp.zeros_like(ct_ref)
        n_ref[...] = jnp.zeros_like(n_ref)
        m_ref[...] = jnp.zeros_like(m_ref)

    x = x_ref[...]
    h = _rms(x, gpre_ref[...]).astype(jnp.bfloat16)
    gc = _dot(h, wg_ref[...]) + brow_ref[...]
    gr = lax.dot_general(wgt_ref[...], h, nt,
                         preferred_element_type=jnp.float32) + bcol_ref[:, 0:1]
    zqk = _dot(h, wqk_ref[...])

    scale = dk ** -0.5
    chunks = range(tm // L)
    heads = range(M_HEADS)

    qb = [[zqk[c * L:(c + 1) * L, j * dk:(j + 1) * dk].astype(jnp.bfloat16) for j in heads]
          for c in chunks]
    kb = [[(zqk[c * L:(c + 1) * L, qk + j * dk: qk + (j + 1) * dk] * scale).astype(jnp.bfloat16)
           for j in heads] for c in chunks]
    kq = [[lax.dot_general(kb[c][j], qb[c][j], nt, preferred_element_type=jnp.float32)
           for j in heads] for c in chunks]
    zvt = lax.dot_general(wvot_ref[:d, :], h, nt, preferred_element_type=jnp.float32)

    acol, brow, b_last, m_a, w_a = {}, {}, {}, {}, {}
    for c in chunks:
        gcc = gc[c * L:(c + 1) * L, :]
        b_c = _cumsum_rows(_log_sigmoid(gcc))
        a_c = pltpu.roll(gcc, M_HEADS, 1) - b_c
        grc = gr[:, c * L:(c + 1) * L]
        b_r = _cumsum_lanes(_log_sigmoid(grc[M_HEADS:, :]))
        for j in heads:
            acol[c, j] = a_c[:, M_HEADS + j: M_HEADS + j + 1]
            brow[c, j] = b_r[j:j + 1, :]
            b_last[c, j] = brow[c, j][:, L - 1:L]
            log_a = b_last[c, j] - brow[c, j] + grc[j:j + 1, :]
            m_a[c, j] = jnp.max(log_a, axis=1, keepdims=True)
            w_a[c, j] = jnp.exp(log_a - m_a[c, j])

    m_loc, s_sum, stb = {}, {}, {}
    for c in chunks:
        for j in heads:
            log_dt = jnp.where(key_le_query, brow[c, j] + acol[c, j], -jnp.inf)
            m_loc[c, j] = jnp.max(log_dt, axis=0, keepdims=True)
            st = kq[c][j] * jnp.exp(log_dt - m_loc[c, j])
            s_sum[c, j] = jnp.sum(st, axis=0, keepdims=True)
            stb[c, j] = st.astype(jnp.bfloat16)

    def state_terms(c, j):
        vt = zvt[j * dv:(j + 1) * dv, c * L:(c + 1) * L]
        w_a_rows = jnp.broadcast_to(w_a[c, j], (BF16_ROWS, L)).astype(jnp.bfloat16)
        return (_dot((vt * w_a[c, j]).astype(jnp.bfloat16), kb[c][j]),
                _dot(w_a_rows, kb[c][j]))

    last = chunks[-1]
    svt, ut, kw_sum = {}, {}, {}
    for c in chunks:
        for j in heads:
            vt = zvt[j * dv:(j + 1) * dv, c * L:(c + 1) * L]
            svt[c, j] = _dot(vt.astype(jnp.bfloat16), stb[c, j])
            if c != last:
                ut[c, j], kw_sum[c, j] = state_terms(c, j)

    ct_in, n_in, m_in, decay = {}, {}, {}, {}
    for j in heads:
        ct_st, n_st, m_st = ct_ref[j], n_ref[j], m_ref[j][0:1, 0:1]
        for c in chunks:
            ct_in[c, j], n_in[c, j], m_in[c, j] = ct_st, n_st, m_st
            m_new = jnp.maximum(b_last[c, j] + m_st, m_a[c, j])
            decay[c, j] = (jnp.exp(b_last[c, j] + m_st - m_new), jnp.exp(m_a[c, j] - m_new))
            if c != last:
                keep, add = decay[c, j]
                ct_st = keep * ct_st + add * ut[c, j]
                n_st = keep * n_st + add * kw_sum[c, j]
            m_st = m_new
        m_ref[j] = jnp.broadcast_to(m_st, m_ref.shape[1:])

    qct = {k: lax.dot_general(ct_in[k].astype(jnp.bfloat16), qb[k[0]][k[1]], nt,
                              preferred_element_type=jnp.float32) for k in ct_in}
    qn = {k: lax.dot_general(n_in[k].astype(jnp.bfloat16), qb[k[0]][k[1]], nt,
                             preferred_element_type=jnp.float32)[0:1, :] for k in n_in}
    zot = lax.dot_general(wvot_ref[d:, :], h, nt, preferred_element_type=jnp.float32)
    for j in heads:
        keep, add = decay[last, j]
        u_last, kw_last = state_terms(last, j)
        ct_ref[j] = keep * ct_in[last, j] + add * u_last
        n_ref[j] = keep * n_in[last, j] + add * kw_last

    for c in chunks:
        rows = slice(c * L, (c + 1) * L)
        gated = []
        for j in heads:
            ot = zot[j * dv:(j + 1) * dv, rows]
            log_inter = brow[c, j] + m_in[c, j]
            m_row = jnp.maximum(log_inter, m_loc[c, j])
            w_inter = jnp.exp(log_inter - m_row)
            w_loc = jnp.exp(m_loc[c, j] - m_row)
            den = w_inter * qn[c, j] + w_loc * s_sum[c, j]
            ht = ((w_inter * qct[c, j] + w_loc * svt[c, j])
                  * (1.0 / jnp.maximum(jnp.abs(den), jnp.exp(-m_row))))
            ht = ht * lax.rsqrt(jnp.mean(ht * ht, axis=0, keepdims=True) + EPS)
            gated.append((ht * hnorm_ref[j * dv:(j + 1) * dv, :] * _sigmoid(ot)).astype(jnp.bfloat16))
        hb = jnp.concatenate([transposed(g) for g in gated], axis=1)
        o_ref[rows, :] = x[rows, :] + _rms(_dot(hb, wout_ref[...]), gpost_ref[...])


def _gelu_tanh(x):
    return 0.5 * x * (1.0 + jnp.tanh(0.7978845608028654 * (x + 0.044715 * (x * x * x))))


def _ffn_ple_body(tiles_per_row, x_ref, p_ref, gfpre_ref, wup_ref, wconv_ref, wdown_ref, gfpost_ref,
                  gple_ref, wgate_ref, wproj_ref, gout_ref, o_ref, tail_ref):
    tm = x_ref.shape[0]
    f = wdown_ref.shape[0]
    rows = [slice(r, r + GROUP_ROWS) for r in range(0, tm, GROUP_ROWS)]
    blocks = [slice(j * COL_BLOCK, (j + 1) * COL_BLOCK) for j in range(f // COL_BLOCK)]
    x = [x_ref[r, :] for r in rows]
    pp = [_dot(p_ref[r, :].astype(jnp.bfloat16), wproj_ref[...]) for r in rows]
    h = [_rms(xi, gfpre_ref[...]).astype(jnp.bfloat16) for xi in x]
    g, u = [], []
    for hi in h:
        gu = [(_dot(hi, wup_ref[:, c]), _dot(hi, wup_ref[:, f + c.start:f + c.stop])) for c in blocks]
        g.append([pair[0] for pair in gu])
        u.append([pair[1] for pair in gu])
    conv = _conv_blocks(g, tail_ref, wconv_ref, _first_of_row(tiles_per_row))
    act = [jnp.concatenate([(_gelu_tanh(cij) * uij).astype(jnp.bfloat16) for cij, uij in zip(ci, ui)], axis=1)
           for ci, ui in zip(conv, u)]
    y = [_dot(ai, wdown_ref[...]) for ai in act]
    x = [xi + _rms(yi, gfpost_ref[...]) for xi, yi in zip(x, y)]
    h = [_rms(xi, gple_ref[...]).astype(jnp.bfloat16) for xi in x]
    gate = [_sigmoid(_dot(hi, wgate_ref[...])) for hi in h]
    for r, xi, gi, pi in zip(rows, x, gate, pp):
        o_ref[r, :] = xi + _rms(gi * pi, gout_ref[...])


def _tile_spec(rows, cols):
    return pl.BlockSpec((rows, cols), lambda i: (i, 0))


def _layer_spec(stacked, layer):
    return pl.BlockSpec((None,) + stacked.shape[1:], lambda i: (layer, 0, 0),
                        pipeline_mode=pl.Buffered(1))


_PARAMS = pltpu.CompilerParams(dimension_semantics=("arbitrary",), vmem_limit_bytes=VMEM_LIMIT_BYTES)


def _conv_mixer(x, seq, layer, mixer, g_pre, w_in, w_conv, w_out, g_post):
    n, d = x.shape
    return pl.pallas_call(
        functools.partial(_conv_mixer_body, seq // TM_CONV),
        grid=(n // TM_CONV,),
        in_specs=[_tile_spec(TM_CONV, d), _layer_spec(g_pre, layer), _layer_spec(w_in, mixer),
                  _layer_spec(w_conv, mixer), _layer_spec(w_out, mixer), _layer_spec(g_post, layer)],
        out_specs=_tile_spec(TM_CONV, d),
        out_shape=jax.ShapeDtypeStruct(x.shape, x.dtype),
        scratch_shapes=[pltpu.VMEM((SUBLANES, d), jnp.float32)],
        compiler_params=_PARAMS,
        name="conv_mixer",
    )(x, g_pre, w_in, w_conv, w_out, g_post)


def _mlstm_mixer(x, seq, layer, mixer, g_pre, per_mixer, g_post):
    n, d = x.shape
    dk = d // 2 // M_HEADS
    dv = d // M_HEADS
    return pl.pallas_call(
        functools.partial(_mlstm_body, seq // TM_MLSTM),
        grid=(n // TM_MLSTM,),
        in_specs=([_tile_spec(TM_MLSTM, d), _layer_spec(g_pre, layer)]
                  + [_layer_spec(a, mixer) for a in per_mixer] + [_layer_spec(g_post, layer)]),
        out_specs=_tile_spec(TM_MLSTM, d),
        out_shape=jax.ShapeDtypeStruct(x.shape, x.dtype),
        scratch_shapes=[pltpu.VMEM((2 * d, d), jnp.bfloat16),
                        pltpu.VMEM((M_HEADS, dv, dk), jnp.float32),
                        pltpu.VMEM((M_HEADS, BF16_ROWS, dk), jnp.float32),
                        pltpu.VMEM((M_HEADS, SUBLANES, LANES), jnp.float32)],
        compiler_params=_PARAMS,
        name="mlstm_mixer",
    )(x, g_pre, *per_mixer, g_post)


def _ffn_ple(x, p, seq, layer, *per_layer):
    n, d = x.shape
    w_down = per_layer[3]
    return pl.pallas_call(
        functools.partial(_ffn_ple_body, seq // TM_FFN),
        grid=(n // TM_FFN,),
        in_specs=([_tile_spec(TM_FFN, d), pl.BlockSpec((None, TM_FFN, p.shape[2]), lambda i: (layer, i, 0))]
                  + [_layer_spec(a, layer) for a in per_layer]),
        out_specs=_tile_spec(TM_FFN, d),
        out_shape=jax.ShapeDtypeStruct(x.shape, x.dtype),
        scratch_shapes=[pltpu.VMEM((SUBLANES, w_down.shape[1]), jnp.float32)],
        compiler_params=_PARAMS,
        name="ffn_ple",
    )(x, p, *per_layer)


def kernel(x, p, norm_mix_pre, norm_mix_post, norm_ffn_pre, norm_ffn_post, sc_w_in, sc_w_conv, sc_w_out, ml_w_in, ml_gate_bias, ml_head_norm, ml_w_out, ffn_w_up, ffn_w_conv, ffn_w_down, ple_norm_gate, ple_w_gate, ple_w_proj, ple_norm_out):
    bsz, seq, d = x.shape
    depth = p.shape[0]
    assert all(seq % t == 0 and t % GROUP_ROWS == 0 for t in (TM_CONV, TM_MLSTM, TM_FFN))
    assert TM_MLSTM % MLSTM_CHUNK == 0
    bf = jnp.bfloat16
    qkvo = 3 * d
    n_gates = 2 * M_HEADS

    def rows(g):
        return g[:, None, :]

    w_g = ml_w_in[:, :, qkvo:]
    w_g_pad = jnp.pad(w_g, ((0, 0), (0, 0), (0, LANES - n_gates))).astype(bf)
    w_gt = jnp.swapaxes(w_g, 1, 2).astype(bf)
    bias = ml_gate_bias.astype(jnp.float32)
    b_row = rows(jnp.pad(bias, ((0, 0), (0, LANES - n_gates))))
    b_col = jnp.broadcast_to(bias[:, :, None], bias.shape + (LANES,))
    h_norm = jnp.broadcast_to(ml_head_norm[:, :, None], ml_head_norm.shape + (MLSTM_CHUNK,))
    mlstm_w = (ml_w_in[:, :, :d].astype(bf), ml_w_in[:, :, d:qkvo].astype(bf), w_g_pad, w_gt, b_row, b_col, h_norm,
               ml_w_out.astype(bf))
    conv_w = (sc_w_in.astype(bf), sc_w_conv, sc_w_out.astype(bf))
    ffn_w = (rows(norm_ffn_pre), ffn_w_up.astype(bf), ffn_w_conv, ffn_w_down.astype(bf),
             rows(norm_ffn_post), rows(ple_norm_gate), ple_w_gate.astype(bf), ple_w_proj.astype(bf),
             rows(ple_norm_out))
    g_pre, g_post = rows(norm_mix_pre), rows(norm_mix_post)

    xf = x.reshape(bsz * seq, d)
    pf = p.reshape(depth, bsz * seq, p.shape[-1])
    for i in range(depth):
        j = i // N_MIXERS
        if i % N_MIXERS == 0:
            xf = _conv_mixer(xf, seq, i, j, g_pre, *conv_w, g_post)
        else:
            xf = _mlstm_mixer(xf, seq, i, j, g_pre, mlstm_w, g_post)
        xf = _ffn_ple(xf, pf, seq, i, *ffn_w)
    return xf.reshape(bsz, seq, d)
```

```python
import functools

import jax
import jax.numpy as jnp
from jax import lax
from jax.experimental import pallas as pl
from jax.experimental.pallas import tpu as pltpu

EPS = 1e-6
N_MIXERS = 2
M_HEADS = 4
CONV_WIDTH = 3
SUBLANES = 8
LANES = 128
BF16_ROWS = 16
COL_BLOCK = 256
GROUP_ROWS = 256
TM_CONV = 2048
TM_MLSTM = 1024
TM_FFN = 1024
MLSTM_CHUNK = 256
VMEM_LIMIT_BYTES = 56 * 1024 * 1024


def _rms(x, g):
    return x * lax.rsqrt(jnp.mean(x * x, axis=-1, keepdims=True) + EPS) * g


def _sigmoid(x):
    return 0.5 * jnp.tanh(0.5 * x) + 0.5


def _causal_conv3(cur, prev, w):
    back1 = prev[SUBLANES - 1:SUBLANES, :]
    back2 = prev[SUBLANES - 2:SUBLANES - 1, :]
    row = lax.broadcasted_iota(jnp.int32, (SUBLANES, cur.shape[1]), 0)
    r1 = pltpu.roll(cur, 1, 0)
    r2 = pltpu.roll(cur, 2, 0)
    h1 = jnp.where(row == 0, back1, r1[:SUBLANES])
    h2 = jnp.where(row == 0, back2, jnp.where(row == 1, back1, r2[:SUBLANES]))
    r1 = jnp.concatenate([h1, r1[SUBLANES:]], axis=0)
    r2 = jnp.concatenate([h2, r2[SUBLANES:]], axis=0)
    return w[0:1, :] * r2 + w[1:2, :] * r1 + w[2:3, :] * cur


def _conv_blocks(pre, tail_ref, wconv_ref, first):
    out = [[] for _ in pre]
    for j in range(len(pre[0])):
        cols = slice(j * COL_BLOCK, (j + 1) * COL_BLOCK)
        prev = jnp.where(first, 0.0, tail_ref[:, cols])
        for i, group in enumerate(pre):
            cur = group[j]
            out[i].append(_causal_conv3(cur, prev, wconv_ref[:, cols]))
            prev = cur[cur.shape[0] - SUBLANES:, :]
        tail_ref[:, cols] = prev
    return out


def _dot(a, b):
    return jnp.dot(a, b, preferred_element_type=jnp.float32)


def _first_of_row(tiles_per_row):
    return pl.program_id(0) % tiles_per_row == 0


def _conv_mixer_body(tiles_per_row, x_ref, gpre_ref, win_ref, wconv_ref, wout_ref, gpost_ref,
                     o_ref, tail_ref):
    tm, d = x_ref.shape
    rows = [slice(r, r + GROUP_ROWS) for r in range(0, tm, GROUP_ROWS)]
    blocks = [slice(j * COL_BLOCK, (j + 1) * COL_BLOCK) for j in range(d // COL_BLOCK)]
    x = [x_ref[r, :] for r in rows]
    h = [_rms(xi, gpre_ref[...]).astype(jnp.bfloat16) for xi in x]
    cu = [[_dot(hi, win_ref[:, d + c.start:d + c.stop]) * _dot(hi, win_ref[:, 2 * d + c.start:2 * d + c.stop])
           for c in blocks] for hi in h]
    b_gate = [_dot(hi, win_ref[:, :d]) for hi in h]
    conv = _conv_blocks(cu, tail_ref, wconv_ref, _first_of_row(tiles_per_row))
    y = [_dot((bi * jnp.concatenate(ci, axis=1)).astype(jnp.bfloat16), wout_ref[...])
         for bi, ci in zip(b_gate, conv)]
    for r, xi, yi in zip(rows, x, y):
        o_ref[r, :] = xi + _rms(yi, gpost_ref[...])


def _log_sigmoid(x):
    return jnp.minimum(x, 0.0) - jnp.log1p(jnp.exp(-jnp.abs(x)))


def _cumsum_rows(x):
    n = x.shape[0]
    row = lax.broadcasted_iota(jnp.int32, x.shape, 0)
    s = 1
    while s < n:
        x = x + jnp.where(row >= s, pltpu.roll(x, s, 0), 0.0)
        s *= 2
    return x


def _cumsum_lanes(x):
    n = x.shape[1]
    col = lax.broadcasted_iota(jnp.int32, x.shape, 1)
    s = 1
    while s < n:
        x = x + jnp.where(col >= s, pltpu.roll(x, s, 1), 0.0)
        s *= 2
    return x


def _mlstm_body(tiles_per_row, x_ref, gpre_ref, wqk_ref, wvo_ref, wg_ref, wgt_ref, brow_ref, bcol_ref,
                hnorm_ref, wout_ref, gpost_ref, o_ref, wvot_ref, ct_ref, n_ref, m_ref):
    d = x_ref.shape[1]
    tm = x_ref.shape[0]
    qk = d // 2
    dk = qk // M_HEADS
    dv = d // M_HEADS
    L = MLSTM_CHUNK
    nt = (((1,), (1,)), ((), ()))

    row_i = lax.broadcasted_iota(jnp.int32, (L, L), 0)
    col_i = lax.broadcasted_iota(jnp.int32, (L, L), 1)
    key_le_query = row_i <= col_i
    eye = (row_i == col_i).astype(jnp.bfloat16)

    def transposed(a):
        return lax.dot_general(eye, a, nt, preferred_element_type=jnp.float32).astype(jnp.bfloat16)

    @pl.when(pl.program_id(0) == 0)
    def _():
        for r in range(wvot_ref.shape[0] // L):
            for c in range(wvot_ref.shape[1] // L):
                wvot_ref[r * L:(r + 1) * L, c * L:(c + 1) * L] = transposed(
                    wvo_ref[c * L:(c + 1) * L, r * L:(r + 1) * L])

    @pl.when(_first_of_row(tiles_per_row))
    def _():
        ct_ref[...] = jnp.zeros_like(ct_ref)
        n_ref[...] = jnp.zeros_like(n_ref)
        m_ref[...] = jnp.zeros_like(m_ref)

    x = x_ref[...]
    h = _rms(x, gpre_ref[...]).astype(jnp.bfloat16)
    gc = _dot(h, wg_ref[...]) + brow_ref[...]
    gr = lax.dot_general(wgt_ref[...], h, nt,
                         preferred_element_type=jnp.float32) + bcol_ref[:, 0:1]
    zqk = _dot(h, wqk_ref[...])

    scale = dk ** -0.5
    chunks = range(tm // L)
    heads = range(M_HEADS)

    qb = [[zqk[c * L:(c + 1) * L, j * dk:(j + 1) * dk].astype(jnp.bfloat16) for j in heads]
          for c in chunks]
    kb = [[(zqk[c * L:(c + 1) * L, qk + j * dk: qk + (j + 1) * dk] * scale).astype(jnp.bfloat16)
           for j in heads] for c in chunks]
    kq = [[lax.dot_general(kb[c][j], qb[c][j], nt, preferred_element_type=jnp.float32)
           for j in heads] for c in chunks]
    zvt = lax.dot_general(wvot_ref[:d, :], h, nt, preferred_element_type=jnp.float32)

    acol, brow, b_last, m_a, w_a = {}, {}, {}, {}, {}
    for c in chunks:
        gcc = gc[c * L:(c + 1) * L, :]
        b_c = _cumsum_rows(_log_sigmoid(gcc))
        a_c = pltpu.roll(gcc, M_HEADS, 1) - b_c
        grc = gr[:, c * L:(c + 1) * L]
        b_r = _cumsum_lanes(_log_sigmoid(grc[M_HEADS:, :]))
        for j in heads:
            acol[c, j] = a_c[:, M_HEADS + j: M_HEADS + j + 1]
            brow[c, j] = b_r[j:j + 1, :]
            b_last[c, j] = brow[c, j][:, L - 1:L]
            log_a = b_last[c, j] - brow[c, j] + grc[j:j + 1, :]
            m_a[c, j] = jnp.max(log_a, axis=1, keepdims=True)
            w_a[c, j] = jnp.exp(log_a - m_a[c, j])

    m_loc, s_sum, stb = {}, {}, {}
    for c in chunks:
        for j in heads:
            log_dt = jnp.where(key_le_query, brow[c, j] + acol[c, j], -jnp.inf)
            m_loc[c, j] = jnp.max(log_dt, axis=0, keepdims=True)
            st = kq[c][j] * jnp.exp(log_dt - m_loc[c, j])
            s_sum[c, j] = jnp.sum(st, axis=0, keepdims=True)
            stb[c, j] = st.astype(jnp.bfloat16)

    def state_terms(c, j):
        vt = zvt[j * dv:(j + 1) * dv, c * L:(c + 1) * L]
        w_a_rows = jnp.broadcast_to(w_a[c, j], (BF16_ROWS, L)).astype(jnp.bfloat16)
        return (_dot((vt * w_a[c, j]).astype(jnp.bfloat16), kb[c][j]),
                _dot(w_a_rows, kb[c][j]))

    last = chunks[-1]
    svt, ut, kw_sum = {}, {}, {}
    for c in chunks:
        for j in heads:
            vt = zvt[j * dv:(j + 1) * dv, c * L:(c + 1) * L]
            svt[c, j] = _dot(vt.astype(jnp.bfloat16), stb[c, j])
            if c != last:
                ut[c, j], kw_sum[c, j] = state_terms(c, j)

    ct_in, n_in, m_in, decay = {}, {}, {}, {}
    for j in heads:
        ct_st, n_st, m_st = ct_ref[j], n_ref[j], m_ref[j][0:1, 0:1]
        for c in chunks:
            ct_in[c, j], n_in[c, j], m_in[c, j] = ct_st, n_st, m_st
            m_new = jnp.maximum(b_last[c, j] + m_st, m_a[c, j])
            decay[c, j] = (jnp.exp(b_last[c, j] + m_st - m_new), jnp.exp(m_a[c, j] - m_new))
            if c != last:
                keep, add = decay[c, j]
                ct_st = keep * ct_st + add * ut[c, j]
                n_st = keep * n_st + add * kw_sum[c, j]
            m_st = m_new
        m_ref[j] = jnp.broadcast_to(m_st, m_ref.shape[1:])

    qct = {k: lax.dot_general(ct_in[k].astype(jnp.bfloat16), qb[k[0]][k[1]], nt,
                              preferred_element_type=jnp.float32) for k in ct_in}
    qn = {k: lax.dot_general(n_in[k].astype(jnp.bfloat16), qb[k[0]][k[1]], nt,
                             preferred_element_type=jnp.float32)[0:1, :] for k in n_in}
    zot = lax.dot_general(wvot_ref[d:, :], h, nt, preferred_element_type=jnp.float32)
    for j in heads:
        keep, add = decay[last, j]
        u_last, kw_last = state_terms(last, j)
        ct_ref[j] = keep * ct_in[last, j] + add * u_last
        n_ref[j] = keep * n_in[last, j] + add * kw_last

    for c in chunks:
        rows = slice(c * L, (c + 1) * L)
        gated = []
        for j in heads:
            ot = zot[j * dv:(j + 1) * dv, rows]
            log_inter = brow[c, j] + m_in[c, j]
            m_row = jnp.maximum(log_inter, m_loc[c, j])
            w_inter = jnp.exp(log_inter - m_row)
            w_loc = jnp.exp(m_loc[c, j] - m_row)
            den = w_inter * qn[c, j] + w_loc * s_sum[c, j]
            ht = ((w_inter * qct[c, j] + w_loc * svt[c, j])
                  * (1.0 / jnp.maximum(jnp.abs(den), jnp.exp(-m_row))))
            ht = ht * lax.rsqrt(jnp.mean(ht * ht, axis=0, keepdims=True) + EPS)
            gated.append((ht * hnorm_ref[j * dv:(j + 1) * dv, :] * _sigmoid(ot)).astype(jnp.bfloat16))
        hb = jnp.concatenate([transposed(g) for g in gated], axis=1)
        o_ref[rows, :] = x[rows, :] + _rms(_dot(hb, wout_ref[...]), gpost_ref[...])


def _gelu_tanh(x):
    return 0.5 * x * (1.0 + jnp.tanh(0.7978845608028654 * (x + 0.044715 * (x * x * x))))


def _ffn_ple_body(tiles_per_row, x_ref, p_ref, gfpre_ref, wup_ref, wconv_ref, wdown_ref, gfpost_ref,
                  gple_ref, wgate_ref, wproj_ref, gout_ref, o_ref, tail_ref):
    tm = x_ref.shape[0]
    f = wdown_ref.shape[0]
    rows = [slice(r, r + GROUP_ROWS) for r in range(0, tm, GROUP_ROWS)]
    blocks = [slice(j * COL_BLOCK, (j + 1) * COL_BLOCK) for j in range(f // COL_BLOCK)]
    x = [x_ref[r, :] for r in rows]
    pp = [_dot(p_ref[r, :].astype(jnp.bfloat16), wproj_ref[...]) for r in rows]
    h = [_rms(xi, gfpre_ref[...]).astype(jnp.bfloat16) for xi in x]
    g, u = [], []
    for hi in h:
        gu = [(_dot(hi, wup_ref[:, c]), _dot(hi, wup_ref[:, f + c.start:f + c.stop])) for c in blocks]
        g.append([pair[0] for pair in gu])
        u.append([pair[1] for pair in gu])
    conv = _conv_blocks(g, tail_ref, wconv_ref, _first_of_row(tiles_per_row))
    act = [jnp.concatenate([(_gelu_tanh(cij) * uij).astype(jnp.bfloat16) for cij, uij in zip(ci, ui)], axis=1)
           for ci, ui in zip(conv, u)]
    y = [_dot(ai, wdown_ref[...]) for ai in act]
    x = [xi + _rms(yi, gfpost_ref[...]) for xi, yi in zip(x, y)]
    h = [_rms(xi, gple_ref[...]).astype(jnp.bfloat16) for xi in x]
    gate = [_sigmoid(_dot(hi, wgate_ref[...])) for hi in h]
    for r, xi, gi, pi in zip(rows, x, gate, pp):
        o_ref[r, :] = xi + _rms(gi * pi, gout_ref[...])


def _tile_spec(rows, cols):
    return pl.BlockSpec((rows, cols), lambda i: (i, 0))


def _layer_spec(stacked, layer):
    return pl.BlockSpec((None,) + stacked.shape[1:], lambda i: (layer, 0, 0),
                        pipeline_mode=pl.Buffered(1))


_PARAMS = pltpu.CompilerParams(dimension_semantics=("arbitrary",), vmem_limit_bytes=VMEM_LIMIT_BYTES)


def _conv_mixer(x, seq, layer, mixer, g_pre, w_in, w_conv, w_out, g_post):
    n, d = x.shape
    return pl.pallas_call(
        functools.partial(_conv_mixer_body, seq // TM_CONV),
        grid=(n // TM_CONV,),
        in_specs=[_tile_spec(TM_CONV, d), _layer_spec(g_pre, layer), _layer_spec(w_in, mixer),
                  _layer_spec(w_conv, mixer), _layer_spec(w_out, mixer), _layer_spec(g_post, layer)],
        out_specs=_tile_spec(TM_CONV, d),
        out_shape=jax.ShapeDtypeStruct(x.shape, x.dtype),
        scratch_shapes=[pltpu.VMEM((SUBLANES, d), jnp.float32)],
        compiler_params=_PARAMS,
        name="conv_mixer",
    )(x, g_pre, w_in, w_conv, w_out, g_post)


def _mlstm_mixer(x, seq, layer, mixer, g_pre, per_mixer, g_post):
    n, d = x.shape
    dk = d // 2 // M_HEADS
    dv = d // M_HEADS
    return pl.pallas_call(
        functools.partial(_mlstm_body, seq // TM_MLSTM),
        grid=(n // TM_MLSTM,),
        in_specs=([_tile_spec(TM_MLSTM, d), _layer_spec(g_pre, layer)]
                  + [_layer_spec(a, mixer) for a in per_mixer] + [_layer_spec(g_post, layer)]),
        out_specs=_tile_spec(TM_MLSTM, d),
        out_shape=jax.ShapeDtypeStruct(x.shape, x.dtype),
        scratch_shapes=[pltpu.VMEM((2 * d, d), jnp.bfloat16),
                        pltpu.VMEM((M_HEADS, dv, dk), jnp.float32),
                        pltpu.VMEM((M_HEADS, BF16_ROWS, dk), jnp.float32),
                        pltpu.VMEM((M_HEADS, SUBLANES, LANES), jnp.float32)],
        compiler_params=_PARAMS,
        name="mlstm_mixer",
    )(x, g_pre, *per_mixer, g_post)


def _ffn_ple(x, p, seq, layer, *per_layer):
    n, d = x.shape
    w_down = per_layer[3]
    return pl.pallas_call(
        functools.partial(_ffn_ple_body, seq // TM_FFN),
        grid=(n // TM_FFN,),
        in_specs=([_tile_spec(TM_FFN, d), pl.BlockSpec((None, TM_FFN, p.shape[2]), lambda i: (layer, i, 0))]
                  + [_layer_spec(a, layer) for a in per_layer]),
        out_specs=_tile_spec(TM_FFN, d),
        out_shape=jax.ShapeDtypeStruct(x.shape, x.dtype),
        scratch_shapes=[pltpu.VMEM((SUBLANES, w_down.shape[1]), jnp.float32)],
        compiler_params=_PARAMS,
        name="ffn_ple",
    )(x, p, *per_layer)


def kernel(x, p, norm_mix_pre, norm_mix_post, norm_ffn_pre, norm_ffn_post, sc_w_in, sc_w_conv, sc_w_out, ml_w_in, ml_gate_bias, ml_head_norm, ml_w_out, ffn_w_up, ffn_w_conv, ffn_w_down, ple_norm_gate, ple_w_gate, ple_w_proj, ple_norm_out):
    bsz, seq, d = x.shape
    depth = p.shape[0]
    assert all(seq % t == 0 and t % GROUP_ROWS == 0 for t in (TM_CONV, TM_MLSTM, TM_FFN))
    assert TM_MLSTM % MLSTM_CHUNK == 0
    bf = jnp.bfloat16
    qkvo = 3 * d
    n_gates = 2 * M_HEADS

    def rows(g):
        return g[:, None, :]

    w_g = ml_w_in[:, :, qkvo:]
    w_g_pad = jnp.pad(w_g, ((0, 0), (0, 0), (0, LANES - n_gates))).astype(bf)
    w_gt = jnp.swapaxes(w_g, 1, 2).astype(bf)
    bias = ml_gate_bias.astype(jnp.float32)
    b_row = rows(jnp.pad(bias, ((0, 0), (0, LANES - n_gates))))
    b_col = jnp.broadcast_to(bias[:, :, None], bias.shape + (LANES,))
    h_norm = jnp.broadcast_to(ml_head_norm[:, :, None], ml_head_norm.shape + (MLSTM_CHUNK,))
    mlstm_w = (ml_w_in[:, :, :d].astype(bf), ml_w_in[:, :, d:qkvo].astype(bf), w_g_pad, w_gt, b_row, b_col, h_norm,
               ml_w_out.astype(bf))
    conv_w = (sc_w_in.astype(bf), sc_w_conv, sc_w_out.astype(bf))
    ffn_w = (rows(norm_ffn_pre), ffn_w_up.astype(bf), ffn_w_conv, ffn_w_down.astype(bf),
             rows(norm_ffn_post), rows(ple_norm_gate), ple_w_gate.astype(bf), ple_w_proj.astype(bf),
             rows(ple_norm_out))
    g_pre, g_post = rows(norm_mix_pre), rows(norm_mix_post)

    xf = x.reshape(bsz * seq, d)
    pf = p.reshape(depth, bsz * seq, p.shape[-1])
    for i in range(depth):
        j = i // N_MIXERS
        if i % N_MIXERS == 0:
            xf = _conv_mixer(xf, seq, i, j, g_pre, *conv_w, g_post)
        else:
            xf = _mlstm_mixer(xf, seq, i, j, g_pre, mlstm_w, g_post)
        xf = _ffn_ple(xf, pf, seq, i, *ffn_w)
    return xf.reshape(bsz, seq, d)
```

```python
import functools

import jax
import jax.numpy as jnp
from jax import lax
from jax.experimental import pallas as pl
from jax.experimental.pallas import tpu as pltpu

EPS = 1e-6
N_MIXERS = 2
M_HEADS = 4
CONV_WIDTH = 3
SUBLANES = 8
LANES = 128
BF16_ROWS = 16
COL_BLOCK = 256
GROUP_ROWS = 256
TM_CONV = 1024
TM_MLSTM = 1024
TM_FFN = 1024
MLSTM_CHUNK = 256
VMEM_LIMIT_BYTES = 56 * 1024 * 1024


def _rms(x, g):
    return x * lax.rsqrt(jnp.mean(x * x, axis=-1, keepdims=True) + EPS) * g


def _sigmoid(x):
    return 0.5 * jnp.tanh(0.5 * x) + 0.5


def _causal_conv3(cur, prev, w):
    back1 = prev[SUBLANES - 1:SUBLANES, :]
    back2 = prev[SUBLANES - 2:SUBLANES - 1, :]
    row = lax.broadcasted_iota(jnp.int32, (SUBLANES, cur.shape[1]), 0)
    r1 = pltpu.roll(cur, 1, 0)
    r2 = pltpu.roll(cur, 2, 0)
    h1 = jnp.where(row == 0, back1, r1[:SUBLANES])
    h2 = jnp.where(row == 0, back2, jnp.where(row == 1, back1, r2[:SUBLANES]))
    r1 = jnp.concatenate([h1, r1[SUBLANES:]], axis=0)
    r2 = jnp.concatenate([h2, r2[SUBLANES:]], axis=0)
    return w[0:1, :] * r2 + w[1:2, :] * r1 + w[2:3, :] * cur


def _conv_blocks(pre, tail_ref, wconv_ref, first):
    out = [[] for _ in pre]
    for j in range(len(pre[0])):
        cols = slice(j * COL_BLOCK, (j + 1) * COL_BLOCK)
        prev = jnp.where(first, 0.0, tail_ref[:, cols])
        for i, group in enumerate(pre):
            cur = group[j]
            out[i].append(_causal_conv3(cur, prev, wconv_ref[:, cols]))
            prev = cur[cur.shape[0] - SUBLANES:, :]
        tail_ref[:, cols] = prev
    return out


def _dot(a, b):
    return jnp.dot(a, b, preferred_element_type=jnp.float32)


def _cast_rows(src_refs, dst_refs):
    for src, dst in zip(src_refs, dst_refs):
        dst[...] = src[...].astype(jnp.bfloat16)


def _first_of_row(tiles_per_row):
    return pl.program_id(0) % tiles_per_row == 0


def _conv_mixer_body(tiles_per_row, n_cast, x_ref, gpre_ref, win_ref, wconv_ref, wout_ref, gpost_ref,
                     *refs):
    cast_src, (o_ref, *cast_dst), (tail_ref,) = refs[:n_cast], refs[n_cast:2 * n_cast + 1], refs[2 * n_cast + 1:]
    _cast_rows(cast_src, cast_dst)
    tm, d = x_ref.shape
    rows = [slice(r, r + GROUP_ROWS) for r in range(0, tm, GROUP_ROWS)]
    blocks = [slice(j * COL_BLOCK, (j + 1) * COL_BLOCK) for j in range(d // COL_BLOCK)]
    x = [x_ref[r, :] for r in rows]
    h = [_rms(xi, gpre_ref[...]).astype(jnp.bfloat16) for xi in x]
    cu = [[_dot(hi, win_ref[:, d + c.start:d + c.stop]) * _dot(hi, win_ref[:, 2 * d + c.start:2 * d + c.stop])
           for c in blocks] for hi in h]
    b_gate = [_dot(hi, win_ref[:, :d]) for hi in h]
    conv = _conv_blocks(cu, tail_ref, wconv_ref, _first_of_row(tiles_per_row))
    y = [_dot((bi * jnp.concatenate(ci, axis=1)).astype(jnp.bfloat16), wout_ref[...])
         for bi, ci in zip(b_gate, conv)]
    for r, xi, yi in zip(rows, x, y):
        o_ref[r, :] = xi + _rms(yi, gpost_ref[...])


def _log_sigmoid(x):
    return jnp.minimum(x, 0.0) - jnp.log1p(jnp.exp(-jnp.abs(x)))


def _cumsum_rows(x):
    n = x.shape[0]
    row = lax.broadcasted_iota(jnp.int32, x.shape, 0)
    s = 1
    while s < n:
        x = x + jnp.where(row >= s, pltpu.roll(x, s, 0), 0.0)
        s *= 2
    return x


def _cumsum_lanes(x):
    n = x.shape[1]
    col = lax.broadcasted_iota(jnp.int32, x.shape, 1)
    s = 1
    while s < n:
        x = x + jnp.where(col >= s, pltpu.roll(x, s, 1), 0.0)
        s *= 2
    return x


def _mlstm_body(tiles_per_row, n_cast, x_ref, gpre_ref, wqk_ref, wvo_ref, wg_ref, wgt_ref, brow_ref, bcol_ref,
                hnorm_ref, wout_ref, gpost_ref, *refs):
    cast_src, (o_ref, *cast_dst) = refs[:n_cast], refs[n_cast:2 * n_cast + 1]
    wvot_ref, ct_ref, n_ref, m_ref = refs[2 * n_cast + 1:]
    _cast_rows(cast_src, cast_dst)
    d = x_ref.shape[1]
    tm = x_ref.shape[0]
    qk = d // 2
    dk = qk // M_HEADS
    dv = d // M_HEADS
    L = MLSTM_CHUNK
    nt = (((1,), (1,)), ((), ()))

    row_i = lax.broadcasted_iota(jnp.int32, (L, L), 0)
    col_i = lax.broadcasted_iota(jnp.int32, (L, L), 1)
    key_le_query = row_i <= col_i
    eye = (row_i == col_i).astype(jnp.bfloat16)

    def transposed(a):
        return lax.dot_general(eye, a, nt, preferred_element_type=jnp.float32).astype(jnp.bfloat16)

    @pl.when(pl.program_id(0) == 0)
    def _():
        for r in range(wvot_ref.shape[0] // L):
            for c in range(wvot_ref.shape[1] // L):
                wvot_ref[r * L:(r + 1) * L, c * L:(c + 1) * L] = transposed(
                    wvo_ref[c * L:(c + 1) * L, r * L:(r + 1) * L])

    @pl.when(_first_of_row(tiles_per_row))
    def _():
        ct_ref[...] = jnp.zeros_like(ct_ref)
        n_ref[...] = jnp.zeros_like(n_ref)
        m_ref[...] = jnp.zeros_like(m_ref)

    x = x_ref[...]
    h = _rms(x, gpre_ref[...]).astype(jnp.bfloat16)
    gc = _dot(h, wg_ref[...]) + brow_ref[...]
    gr = lax.dot_general(wgt_ref[...], h, nt,
                         preferred_element_type=jnp.float32) + bcol_ref[:, 0:1]
    zqk = _dot(h, wqk_ref[...])

    scale = dk ** -0.5
    chunks = range(tm // L)
    heads = range(M_HEADS)

    qb = [[zqk[c * L:(c + 1) * L, j * dk:(j + 1) * dk].astype(jnp.bfloat16) for j in heads]
          for c in chunks]
    kb = [[(zqk[c * L:(c + 1) * L, qk + j * dk: qk + (j + 1) * dk] * scale).astype(jnp.bfloat16)
           for j in heads] for c in chunks]
    kq = [[lax.dot_general(kb[c][j], qb[c][j], nt, preferred_element_type=jnp.float32)
           for j in heads] for c in chunks]
    zvt = lax.dot_general(wvot_ref[:d, :], h, nt, preferred_element_type=jnp.float32)

    acol, brow, b_last, m_a, w_a = {}, {}, {}, {}, {}
    for c in chunks:
        gcc = gc[c * L:(c + 1) * L, :]
        b_c = _cumsum_rows(_log_sigmoid(gcc))
        a_c = pltpu.roll(gcc, M_HEADS, 1) - b_c
        grc = gr[:, c * L:(c + 1) * L]
        b_r = _cumsum_lanes(_log_sigmoid(grc[M_HEADS:, :]))
        for j in heads:
            acol[c, j] = a_c[:, M_HEADS + j: M_HEADS + j + 1]
            brow[c, j] = b_r[j:j + 1, :]
            b_last[c, j] = brow[c, j][:, L - 1:L]
            log_a = b_last[c, j] - brow[c, j] + grc[j:j + 1, :]
            m_a[c, j] = jnp.max(log_a, axis=1, keepdims=True)
            w_a[c, j] = jnp.exp(log_a - m_a[c, j])

    m_loc, s_sum, stb = {}, {}, {}
    for c in chunks:
        for j in heads:
            log_dt = jnp.where(key_le_query, brow[c, j] + acol[c, j], -jnp.inf)
            m_loc[c, j] = jnp.max(log_dt, axis=0, keepdims=True)
            st = kq[c][j] * jnp.exp(log_dt - m_loc[c, j])
            s_sum[c, j] = jnp.sum(st, axis=0, keepdims=True)
            stb[c, j] = st.astype(jnp.bfloat16)

    def state_terms(c, j):
        vt = zvt[j * dv:(j + 1) * dv, c * L:(c + 1) * L]
        w_a_rows = jnp.broadcast_to(w_a[c, j], (BF16_ROWS, L)).astype(jnp.bfloat16)
        return (_dot((vt * w_a[c, j]).astype(jnp.bfloat16), kb[c][j]),
                _dot(w_a_rows, kb[c][j]))

    last = chunks[-1]
    svt, ut, kw_sum = {}, {}, {}
    for c in chunks:
        for j in heads:
            vt = zvt[j * dv:(j + 1) * dv, c * L:(c + 1) * L]
            svt[c, j] = _dot(vt.astype(jnp.bfloat16), stb[c, j])
            if c != last:
                ut[c, j], kw_sum[c, j] = state_terms(c, j)

    ct_in, n_in, m_in, decay = {}, {}, {}, {}
    for j in heads:
        ct_st, n_st, m_st = ct_ref[j], n_ref[j], m_ref[j][0:1, 0:1]
        for c in chunks:
            ct_in[c, j], n_in[c, j], m_in[c, j] = ct_st, n_st, m_st
            m_new = jnp.maximum(b_last[c, j] + m_st, m_a[c, j])
            decay[c, j] = (jnp.exp(b_last[c, j] + m_st - m_new), jnp.exp(m_a[c, j] - m_new))
            if c != last:
                keep, add = decay[c, j]
                ct_st = keep * ct_st + add * ut[c, j]
                n_st = keep * n_st + add * kw_sum[c, j]
            m_st = m_new
        m_ref[j] = jnp.broadcast_to(m_st, m_ref.shape[1:])

    qct = {k: lax.dot_general(ct_in[k].astype(jnp.bfloat16), qb[k[0]][k[1]], nt,
                              preferred_element_type=jnp.float32) for k in ct_in}
    qn = {k: lax.dot_general(n_in[k].astype(jnp.bfloat16), qb[k[0]][k[1]], nt,
                             preferred_element_type=jnp.float32)[0:1, :] for k in n_in}
    zot = lax.dot_general(wvot_ref[d:, :], h, nt, preferred_element_type=jnp.float32)
    for j in heads:
        keep, add = decay[last, j]
        u_last, kw_last = state_terms(last, j)
        ct_ref[j] = keep * ct_in[last, j] + add * u_last
        n_ref[j] = keep * n_in[last, j] + add * kw_last

    for c in chunks:
        rows = slice(c * L, (c + 1) * L)
        gated = []
        for j in heads:
            ot = zot[j * dv:(j + 1) * dv, rows]
            log_inter = brow[c, j] + m_in[c, j]
            m_row = jnp.maximum(log_inter, m_loc[c, j])
            w_inter = jnp.exp(log_inter - m_row)
            w_loc = jnp.exp(m_loc[c, j] - m_row)
            den = w_inter * qn[c, j] + w_loc * s_sum[c, j]
            ht = ((w_inter * qct[c, j] + w_loc * svt[c, j])
                  * (1.0 / jnp.maximum(jnp.abs(den), jnp.exp(-m_row))))
            ht = ht * lax.rsqrt(jnp.mean(ht * ht, axis=0, keepdims=True) + EPS)
            gated.append((ht * hnorm_ref[j * dv:(j + 1) * dv, :] * _sigmoid(ot)).astype(jnp.bfloat16))
        hb = jnp.concatenate([transposed(g) for g in gated], axis=1)
        o_ref[rows, :] = x[rows, :] + _rms(_dot(hb, wout_ref[...]), gpost_ref[...])


def _gelu_tanh(x):
    return 0.5 * x * (1.0 + jnp.tanh(0.7978845608028654 * (x + 0.044715 * (x * x * x))))


def _ffn_ple_body(tiles_per_row, x_ref, p_ref, gfpre_ref, wup_ref, wconv_ref, wdown_ref, gfpost_ref,
                  gple_ref, wgate_ref, wproj_ref, gout_ref, o_ref, tail_ref):
    tm = x_ref.shape[0]
    f = wdown_ref.shape[0]
    rows = [slice(r, r + GROUP_ROWS) for r in range(0, tm, GROUP_ROWS)]
    blocks = [slice(j * COL_BLOCK, (j + 1) * COL_BLOCK) for j in range(f // COL_BLOCK)]
    x = [x_ref[r, :] for r in rows]
    pp = [_dot(p_ref[r, :].astype(jnp.bfloat16), wproj_ref[...]) for r in rows]
    h = [_rms(xi, gfpre_ref[...]).astype(jnp.bfloat16) for xi in x]
    g, u = [], []
    for hi in h:
        gu = [(_dot(hi, wup_ref[:, c]), _dot(hi, wup_ref[:, f + c.start:f + c.stop])) for c in blocks]
        g.append([pair[0] for pair in gu])
        u.append([pair[1] for pair in gu])
    conv = _conv_blocks(g, tail_ref, wconv_ref, _first_of_row(tiles_per_row))
    act = [jnp.concatenate([(_gelu_tanh(cij) * uij).astype(jnp.bfloat16) for cij, uij in zip(ci, ui)], axis=1)
           for ci, ui in zip(conv, u)]
    y = [_dot(ai, wdown_ref[...]) for ai in act]
    x = [xi + _rms(yi, gfpost_ref[...]) for xi, yi in zip(x, y)]
    h = [_rms(xi, gple_ref[...]).astype(jnp.bfloat16) for xi in x]
    gate = [_sigmoid(_dot(hi, wgate_ref[...])) for hi in h]
    for r, xi, gi, pi in zip(rows, x, gate, pp):
        o_ref[r, :] = xi + _rms(gi * pi, gout_ref[...])


def _tile_spec(rows, cols):
    return pl.BlockSpec((rows, cols), lambda i: (i, 0))


def _layer_spec(stacked, layer):
    return pl.BlockSpec((None,) + stacked.shape[1:], lambda i: (layer, 0, 0),
                        pipeline_mode=pl.Buffered(1))


_PARAMS = pltpu.CompilerParams(dimension_semantics=("arbitrary",), vmem_limit_bytes=VMEM_LIMIT_BYTES)


def _cast_rider(stacked, layer, steps):
    _, r, c = stacked.shape
    slabs = next(n for n in range(min(steps, r // BF16_ROWS), 0, -1)
                 if r % n == 0 and (r // n) % BF16_ROWS == 0)
    block = (None, r // slabs, c)
    return (pl.BlockSpec(block, lambda i: (layer, jnp.minimum(i, slabs - 1), 0)),
            pl.BlockSpec(block, lambda i: (0, jnp.minimum(i, slabs - 1), 0)),
            jax.ShapeDtypeStruct((1, r, c), jnp.bfloat16))


def _conv_mixer(x, seq, layer, mixer, g_pre, w_in, w_conv, w_out, g_post, to_cast):
    n, d = x.shape
    steps = n // TM_CONV
    riders = [_cast_rider(w, layer, steps) for w in to_cast]
    out = pl.pallas_call(
        functools.partial(_conv_mixer_body, seq // TM_CONV, len(riders)),
        grid=(steps,),
        in_specs=[_tile_spec(TM_CONV, d), _layer_spec(g_pre, layer), _layer_spec(w_in, mixer),
                  _layer_spec(w_conv, mixer), _layer_spec(w_out, mixer), _layer_spec(g_post, layer)]
        + [r[0] for r in riders],
        out_specs=[_tile_spec(TM_CONV, d)] + [r[1] for r in riders],
        out_shape=[jax.ShapeDtypeStruct(x.shape, x.dtype)] + [r[2] for r in riders],
        scratch_shapes=[pltpu.VMEM((SUBLANES, d), jnp.float32)],
        compiler_params=_PARAMS,
        name="conv_mixer",
    )(x, g_pre, w_in, w_conv, w_out, g_post, *to_cast)
    return out[0], out[1:]


def _mlstm_mixer(x, seq, layer, mixer, g_pre, per_mixer, g_post, to_cast):
    n, d = x.shape
    dk = d // 2 // M_HEADS
    dv = d // M_HEADS
    steps = n // TM_MLSTM
    riders = [_cast_rider(w, layer, steps) for w in to_cast]
    out = pl.pallas_call(
        functools.partial(_mlstm_body, seq // TM_MLSTM, len(riders)),
        grid=(steps,),
        in_specs=([_tile_spec(TM_MLSTM, d), _layer_spec(g_pre, layer)]
                  + [_layer_spec(a, mixer) for a in per_mixer] + [_layer_spec(g_post, layer)]
                  + [r[0] for r in riders]),
        out_specs=[_tile_spec(TM_MLSTM, d)] + [r[1] for r in riders],
        out_shape=[jax.ShapeDtypeStruct(x.shape, x.dtype)] + [r[2] for r in riders],
        scratch_shapes=[pltpu.VMEM((2 * d, d), jnp.bfloat16),
                        pltpu.VMEM((M_HEADS, dv, dk), jnp.float32),
                        pltpu.VMEM((M_HEADS, BF16_ROWS, dk), jnp.float32),
                        pltpu.VMEM((M_HEADS, SUBLANES, LANES), jnp.float32)],
        compiler_params=_PARAMS,
        name="mlstm_mixer",
    )(x, g_pre, *per_mixer, g_post, *to_cast)
    return out[0], out[1:]


def _ffn_ple(x, p, seq, layer, *per_layer):
    n, d = x.shape
    w_down = per_layer[3][0]
    return pl.pallas_call(
        functools.partial(_ffn_ple_body, seq // TM_FFN),
        grid=(n // TM_FFN,),
        in_specs=([_tile_spec(TM_FFN, d), pl.BlockSpec((None, TM_FFN, p.shape[2]), lambda i: (layer, i, 0))]
                  + [_layer_spec(a, at) for a, at in per_layer]),
        out_specs=_tile_spec(TM_FFN, d),
        out_shape=jax.ShapeDtypeStruct(x.shape, x.dtype),
        scratch_shapes=[pltpu.VMEM((SUBLANES, w_down.shape[1]), jnp.float32)],
        compiler_params=_PARAMS,
        name="ffn_ple",
    )(x, p, *[a for a, _ in per_layer])


def kernel(x, p, norm_mix_pre, norm_mix_post, norm_ffn_pre, norm_ffn_post, sc_w_in, sc_w_conv, sc_w_out, ml_w_in, ml_gate_bias, ml_head_norm, ml_w_out, ffn_w_up, ffn_w_conv, ffn_w_down, ple_norm_gate, ple_w_gate, ple_w_proj, ple_norm_out):
    bsz, seq, d = x.shape
    depth = p.shape[0]
    assert all(seq % t == 0 and t % GROUP_ROWS == 0 for t in (TM_CONV, TM_MLSTM, TM_FFN))
    assert TM_MLSTM % MLSTM_CHUNK == 0
    bf = jnp.bfloat16
    qkvo = 3 * d
    n_gates = 2 * M_HEADS

    def rows(g):
        return g[:, None, :]

    w_g = ml_w_in[:, :, qkvo:]
    w_g_pad = jnp.pad(w_g, ((0, 0), (0, 0), (0, LANES - n_gates))).astype(bf)
    w_gt = jnp.swapaxes(w_g, 1, 2).astype(bf)
    bias = ml_gate_bias.astype(jnp.float32)
    b_row = rows(jnp.pad(bias, ((0, 0), (0, LANES - n_gates))))
    b_col = jnp.broadcast_to(bias[:, :, None], bias.shape + (LANES,))
    h_norm = jnp.broadcast_to(ml_head_norm[:, :, None], ml_head_norm.shape + (MLSTM_CHUNK,))
    mlstm_w = (ml_w_in[:, :, :d].astype(bf), ml_w_in[:, :, d:qkvo].astype(bf), w_g_pad, w_gt, b_row, b_col, h_norm,
               ml_w_out.astype(bf))
    conv_w = (sc_w_in.astype(bf), sc_w_conv, sc_w_out.astype(bf))
    ffn_f32 = (ffn_w_up, ffn_w_down, ple_w_gate, ple_w_proj)
    g_pre, g_post = rows(norm_mix_pre), rows(norm_mix_post)

    xf = x.reshape(bsz * seq, d)
    pf = p.reshape(depth, bsz * seq, p.shape[-1])
    for i in range(depth):
        j = i // N_MIXERS
        if i % N_MIXERS == 0:
            xf, (w_up, w_down, w_gate, w_proj) = _conv_mixer(xf, seq, i, j, g_pre, *conv_w, g_post, ffn_f32)
        else:
            xf, (w_up, w_down, w_gate, w_proj) = _mlstm_mixer(xf, seq, i, j, g_pre, mlstm_w, g_post, ffn_f32)
        xf = _ffn_ple(xf, pf, seq, i, (rows(norm_ffn_pre), i), (w_up, 0), (ffn_w_conv, i), (w_down, 0),
                      (rows(norm_ffn_post), i), (rows(ple_norm_gate), i), (w_gate, 0), (w_proj, 0),
                      (rows(ple_norm_out), i))
    return xf.reshape(bsz, seq, d)
```

```python
import functools

import jax
import jax.numpy as jnp
from jax import lax
from jax.experimental import pallas as pl
from jax.experimental.pallas import tpu as pltpu

EPS = 1e-6
N_MIXERS = 2
M_HEADS = 4
CONV_WIDTH = 3
SUBLANES = 8
LANES = 128
BF16_ROWS = 16
COL_BLOCK = 256
GROUP_ROWS = 256
TM_CONV = 1024
TM_MLSTM = 1024
TM_FFN = 1024
MLSTM_CHUNK = 256
VMEM_LIMIT_BYTES = 56 * 1024 * 1024


def _rms(x, g):
    return x * lax.rsqrt(jnp.mean(x * x, axis=-1, keepdims=True) + EPS) * g


def _sigmoid(x):
    return 0.5 * jnp.tanh(0.5 * x) + 0.5


def _causal_conv3(cur, prev, w):
    back1 = prev[SUBLANES - 1:SUBLANES, :]
    back2 = prev[SUBLANES - 2:SUBLANES - 1, :]
    row = lax.broadcasted_iota(jnp.int32, (SUBLANES, cur.shape[1]), 0)
    r1 = pltpu.roll(cur, 1, 0)
    r2 = pltpu.roll(cur, 2, 0)
    h1 = jnp.where(row == 0, back1, r1[:SUBLANES])
    h2 = jnp.where(row == 0, back2, jnp.where(row == 1, back1, r2[:SUBLANES]))
    r1 = jnp.concatenate([h1, r1[SUBLANES:]], axis=0)
    r2 = jnp.concatenate([h2, r2[SUBLANES:]], axis=0)
    return w[0:1, :] * r2 + w[1:2, :] * r1 + w[2:3, :] * cur


def _conv_blocks(pre, tail_ref, wconv_ref, first):
    out = [[] for _ in pre]
    for j in range(len(pre[0])):
        cols = slice(j * COL_BLOCK, (j + 1) * COL_BLOCK)
        prev = jnp.where(first, 0.0, tail_ref[:, cols])
        for i, group in enumerate(pre):
            cur = group[j]
            out[i].append(_causal_conv3(cur, prev, wconv_ref[:, cols]))
            prev = cur[cur.shape[0] - SUBLANES:, :]
        tail_ref[:, cols] = prev
    return out


def _dot(a, b):
    return jnp.dot(a, b, preferred_element_type=jnp.float32)


def _cast_rows(src_refs, dst_refs):
    for src, dst in zip(src_refs, dst_refs):
        dst[...] = src[...].astype(jnp.bfloat16)


def _first_of_row(tiles_per_row):
    return pl.program_id(0) % tiles_per_row == 0


def _conv_mixer_body(tiles_per_row, n_cast, x_ref, gpre_ref, win_ref, wconv_ref, wout_ref, gpost_ref,
                     *refs):
    cast_src, (o_ref, *cast_dst), (tail_ref,) = refs[:n_cast], refs[n_cast:2 * n_cast + 1], refs[2 * n_cast + 1:]
    _cast_rows(cast_src, cast_dst)
    tm, d = x_ref.shape
    rows = [slice(r, r + GROUP_ROWS) for r in range(0, tm, GROUP_ROWS)]
    blocks = [slice(j * COL_BLOCK, (j + 1) * COL_BLOCK) for j in range(d // COL_BLOCK)]
    x = [x_ref[r, :] for r in rows]
    h = [_rms(xi, gpre_ref[...]).astype(jnp.bfloat16) for xi in x]
    cu = [[_dot(hi, win_ref[:, d + c.start:d + c.stop]) * _dot(hi, win_ref[:, 2 * d + c.start:2 * d + c.stop])
           for c in blocks] for hi in h]
    b_gate = [_dot(hi, win_ref[:, :d]) for hi in h]
    conv = _conv_blocks(cu, tail_ref, wconv_ref, _first_of_row(tiles_per_row))
    y = [_dot((bi * jnp.concatenate(ci, axis=1)).astype(jnp.bfloat16), wout_ref[...])
         for bi, ci in zip(b_gate, conv)]
    for r, xi, yi in zip(rows, x, y):
        o_ref[r, :] = xi + _rms(yi, gpost_ref[...])


def _log_sigmoid(x):
    return jnp.minimum(x, 0.0) - jnp.log1p(jnp.exp(-jnp.abs(x)))


def _cumsum_rows(x):
    n = x.shape[0]
    row = lax.broadcasted_iota(jnp.int32, x.shape, 0)
    s = 1
    while s < n:
        x = x + jnp.where(row >= s, pltpu.roll(x, s, 0), 0.0)
        s *= 2
    return x


def _cumsum_lanes(x):
    n = x.shape[1]
    col = lax.broadcasted_iota(jnp.int32, x.shape, 1)
    s = 1
    while s < n:
        x = x + jnp.where(col >= s, pltpu.roll(x, s, 1), 0.0)
        s *= 2
    return x


def _mlstm_body(tiles_per_row, n_cast, x_ref, gpre_ref, win_ref, wg_ref, wgt_ref, brow_ref, bcol_ref,
                hnorm_ref, wout_ref, gpost_ref, *refs):
    cast_src, (o_ref, *cast_dst) = refs[:n_cast], refs[n_cast:2 * n_cast + 1]
    wvot_ref, ct_ref, n_ref, m_ref = refs[2 * n_cast + 1:]
    _cast_rows(cast_src, cast_dst)
    d = x_ref.shape[1]
    tm = x_ref.shape[0]
    qk = d // 2
    dk = qk // M_HEADS
    dv = d // M_HEADS
    L = MLSTM_CHUNK
    nt = (((1,), (1,)), ((), ()))

    row_i = lax.broadcasted_iota(jnp.int32, (L, L), 0)
    col_i = lax.broadcasted_iota(jnp.int32, (L, L), 1)
    key_le_query = row_i <= col_i
    eye = (row_i == col_i).astype(jnp.bfloat16)

    def transposed(a):
        return lax.dot_general(eye, a, nt, preferred_element_type=jnp.float32).astype(jnp.bfloat16)

    @pl.when(pl.program_id(0) == 0)
    def _():
        for r in range(wvot_ref.shape[0] // L):
            for c in range(wvot_ref.shape[1] // L):
                wvot_ref[r * L:(r + 1) * L, c * L:(c + 1) * L] = transposed(
                    win_ref[c * L:(c + 1) * L, d + r * L:d + (r + 1) * L])

    @pl.when(_first_of_row(tiles_per_row))
    def _():
        ct_ref[...] = jnp.zeros_like(ct_ref)
        n_ref[...] = jnp.zeros_like(n_ref)
        m_ref[...] = jnp.zeros_like(m_ref)

    x = x_ref[...]
    h = _rms(x, gpre_ref[...]).astype(jnp.bfloat16)
    gc = _dot(h, wg_ref[...]) + brow_ref[...]
    gr = lax.dot_general(wgt_ref[...], h, nt,
                         preferred_element_type=jnp.float32) + bcol_ref[:, 0:1]
    zqk = _dot(h, win_ref[:, :d])

    scale = dk ** -0.5
    chunks = range(tm // L)
    heads = range(M_HEADS)

    qb = [[zqk[c * L:(c + 1) * L, j * dk:(j + 1) * dk].astype(jnp.bfloat16) for j in heads]
          for c in chunks]
    kb = [[(zqk[c * L:(c + 1) * L, qk + j * dk: qk + (j + 1) * dk] * scale).astype(jnp.bfloat16)
           for j in heads] for c in chunks]
    kq = [[lax.dot_general(kb[c][j], qb[c][j], nt, preferred_element_type=jnp.float32)
           for j in heads] for c in chunks]
    zvt = lax.dot_general(wvot_ref[:d, :], h, nt, preferred_element_type=jnp.float32)

    acol, brow, b_last, m_a, w_a = {}, {}, {}, {}, {}
    for c in chunks:
        gcc = gc[c * L:(c + 1) * L, :]
        b_c = _cumsum_rows(_log_sigmoid(gcc))
        a_c = pltpu.roll(gcc, M_HEADS, 1) - b_c
        grc = gr[:, c * L:(c + 1) * L]
        b_r = _cumsum_lanes(_log_sigmoid(grc[M_HEADS:, :]))
        for j in heads:
            acol[c, j] = a_c[:, M_HEADS + j: M_HEADS + j + 1]
            brow[c, j] = b_r[j:j + 1, :]
            b_last[c, j] = brow[c, j][:, L - 1:L]
            log_a = b_last[c, j] - brow[c, j] + grc[j:j + 1, :]
            m_a[c, j] = jnp.max(log_a, axis=1, keepdims=True)
            w_a[c, j] = jnp.exp(log_a - m_a[c, j])

    m_loc, s_sum, stb = {}, {}, {}
    for c in chunks:
        for j in heads:
            log_dt = jnp.where(key_le_query, brow[c, j] + acol[c, j], -jnp.inf)
            m_loc[c, j] = jnp.max(log_dt, axis=0, keepdims=True)
            st = kq[c][j] * jnp.exp(log_dt - m_loc[c, j])
            s_sum[c, j] = jnp.sum(st, axis=0, keepdims=True)
            stb[c, j] = st.astype(jnp.bfloat16)

    def state_terms(c, j):
        vt = zvt[j * dv:(j + 1) * dv, c * L:(c + 1) * L]
        w_a_rows = jnp.broadcast_to(w_a[c, j], (BF16_ROWS, L)).astype(jnp.bfloat16)
        return (_dot((vt * w_a[c, j]).astype(jnp.bfloat16), kb[c][j]),
                _dot(w_a_rows, kb[c][j]))

    last = chunks[-1]
    svt, ut, kw_sum = {}, {}, {}
    for c in chunks:
        for j in heads:
            vt = zvt[j * dv:(j + 1) * dv, c * L:(c + 1) * L]
            svt[c, j] = _dot(vt.astype(jnp.bfloat16), stb[c, j])
            if c != last:
                ut[c, j], kw_sum[c, j] = state_terms(c, j)

    ct_in, n_in, m_in, decay = {}, {}, {}, {}
    for j in heads:
        ct_st, n_st, m_st = ct_ref[j], n_ref[j], m_ref[j][0:1, 0:1]
        for c in chunks:
            ct_in[c, j], n_in[c, j], m_in[c, j] = ct_st, n_st, m_st
            m_new = jnp.maximum(b_last[c, j] + m_st, m_a[c, j])
            decay[c, j] = (jnp.exp(b_last[c, j] + m_st - m_new), jnp.exp(m_a[c, j] - m_new))
            if c != last:
                keep, add = decay[c, j]
                ct_st = keep * ct_st + add * ut[c, j]
                n_st = keep * n_st + add * kw_sum[c, j]
            m_st = m_new
        m_ref[j] = jnp.broadcast_to(m_st, m_ref.shape[1:])

    qct = {k: lax.dot_general(ct_in[k].astype(jnp.bfloat16), qb[k[0]][k[1]], nt,
                              preferred_element_type=jnp.float32) for k in ct_in}
    qn = {k: lax.dot_general(n_in[k].astype(jnp.bfloat16), qb[k[0]][k[1]], nt,
                             preferred_element_type=jnp.float32)[0:1, :] for k in n_in}
    zot = lax.dot_general(wvot_ref[d:, :], h, nt, preferred_element_type=jnp.float32)
    for j in heads:
        keep, add = decay[last, j]
        u_last, kw_last = state_terms(last, j)
        ct_ref[j] = keep * ct_in[last, j] + add * u_last
        n_ref[j] = keep * n_in[last, j] + add * kw_last

    for c in chunks:
        rows = slice(c * L, (c + 1) * L)
        gated = []
        for j in heads:
            ot = zot[j * dv:(j + 1) * dv, rows]
            log_inter = brow[c, j] + m_in[c, j]
            m_row = jnp.maximum(log_inter, m_loc[c, j])
            w_inter = jnp.exp(log_inter - m_row)
            w_loc = jnp.exp(m_loc[c, j] - m_row)
            den = w_inter * qn[c, j] + w_loc * s_sum[c, j]
            ht = ((w_inter * qct[c, j] + w_loc * svt[c, j])
                  * (1.0 / jnp.maximum(jnp.abs(den), jnp.exp(-m_row))))
            ht = ht * lax.rsqrt(jnp.mean(ht * ht, axis=0, keepdims=True) + EPS)
            gated.append((ht * hnorm_ref[j * dv:(j + 1) * dv, :] * _sigmoid(ot)).astype(jnp.bfloat16))
        hb = jnp.concatenate([transposed(g) for g in gated], axis=1)
        o_ref[rows, :] = x[rows, :] + _rms(_dot(hb, wout_ref[...]), gpost_ref[...])


def _gelu_tanh(x):
    return 0.5 * x * (1.0 + jnp.tanh(0.7978845608028654 * (x + 0.044715 * (x * x * x))))


def _ffn_ple_body(tiles_per_row, x_ref, p_ref, gfpre_ref, wup_ref, wconv_ref, wdown_ref, gfpost_ref,
                  gple_ref, wgate_ref, wproj_ref, gout_ref, o_ref, tail_ref):
    tm = x_ref.shape[0]
    f = wdown_ref.shape[0]
    rows = [slice(r, r + GROUP_ROWS) for r in range(0, tm, GROUP_ROWS)]
    blocks = [slice(j * COL_BLOCK, (j + 1) * COL_BLOCK) for j in range(f // COL_BLOCK)]
    x = [x_ref[r, :] for r in rows]
    pp = [_dot(p_ref[r, :].astype(jnp.bfloat16), wproj_ref[...]) for r in rows]
    h = [_rms(xi, gfpre_ref[...]).astype(jnp.bfloat16) for xi in x]
    g, u = [], []
    for hi in h:
        gu = [(_dot(hi, wup_ref[:, c]), _dot(hi, wup_ref[:, f + c.start:f + c.stop])) for c in blocks]
        g.append([pair[0] for pair in gu])
        u.append([pair[1] for pair in gu])
    conv = _conv_blocks(g, tail_ref, wconv_ref, _first_of_row(tiles_per_row))
    act = [jnp.concatenate([(_gelu_tanh(cij) * uij).astype(jnp.bfloat16) for cij, uij in zip(ci, ui)], axis=1)
           for ci, ui in zip(conv, u)]
    y = [_dot(ai, wdown_ref[...]) for ai in act]
    x = [xi + _rms(yi, gfpost_ref[...]) for xi, yi in zip(x, y)]
    h = [_rms(xi, gple_ref[...]).astype(jnp.bfloat16) for xi in x]
    gate = [_sigmoid(_dot(hi, wgate_ref[...])) for hi in h]
    for r, xi, gi, pi in zip(rows, x, gate, pp):
        o_ref[r, :] = xi + _rms(gi * pi, gout_ref[...])


def _tile_spec(rows, cols):
    return pl.BlockSpec((rows, cols), lambda i: (i, 0))


def _layer_spec(stacked, layer):
    return pl.BlockSpec((None,) + stacked.shape[1:], lambda i: (layer, 0, 0),
                        pipeline_mode=pl.Buffered(1))


_PARAMS = pltpu.CompilerParams(dimension_semantics=("arbitrary",), vmem_limit_bytes=VMEM_LIMIT_BYTES)


def _cast_rider(stacked, layer, steps):
    _, r, c = stacked.shape
    slabs = next(n for n in range(min(steps, r // BF16_ROWS), 0, -1)
                 if r % n == 0 and (r // n) % BF16_ROWS == 0)
    block = (None, r // slabs, c)
    return (pl.BlockSpec(block, lambda i: (layer, jnp.minimum(i, slabs - 1), 0)),
            pl.BlockSpec(block, lambda i: (0, jnp.minimum(i, slabs - 1), 0)),
            jax.ShapeDtypeStruct((1, r, c), jnp.bfloat16))


def _conv_mixer(x, seq, layer, g_pre, w_in, w_conv, w_out, g_post, to_cast):
    n, d = x.shape
    steps = n // TM_CONV
    riders = [_cast_rider(w, at, steps) for w, at in to_cast]
    out = pl.pallas_call(
        functools.partial(_conv_mixer_body, seq // TM_CONV, len(riders)),
        grid=(steps,),
        in_specs=[_tile_spec(TM_CONV, d), _layer_spec(g_pre, layer), _layer_spec(*w_in),
                  _layer_spec(*w_conv), _layer_spec(*w_out), _layer_spec(g_post, layer)]
        + [r[0] for r in riders],
        out_specs=[_tile_spec(TM_CONV, d)] + [r[1] for r in riders],
        out_shape=[jax.ShapeDtypeStruct(x.shape, x.dtype)] + [r[2] for r in riders],
        scratch_shapes=[pltpu.VMEM((SUBLANES, d), jnp.float32)],
        compiler_params=_PARAMS,
        name="conv_mixer",
    )(x, g_pre, w_in[0], w_conv[0], w_out[0], g_post, *[w for w, _ in to_cast])
    return out[0], out[1:]


def _mlstm_mixer(x, seq, layer, g_pre, per_mixer, g_post, to_cast):
    n, d = x.shape
    dk = d // 2 // M_HEADS
    dv = d // M_HEADS
    steps = n // TM_MLSTM
    riders = [_cast_rider(w, at, steps) for w, at in to_cast]
    out = pl.pallas_call(
        functools.partial(_mlstm_body, seq // TM_MLSTM, len(riders)),
        grid=(steps,),
        in_specs=([_tile_spec(TM_MLSTM, d), _layer_spec(g_pre, layer)]
                  + [_layer_spec(a, at) for a, at in per_mixer] + [_layer_spec(g_post, layer)]
                  + [r[0] for r in riders]),
        out_specs=[_tile_spec(TM_MLSTM, d)] + [r[1] for r in riders],
        out_shape=[jax.ShapeDtypeStruct(x.shape, x.dtype)] + [r[2] for r in riders],
        scratch_shapes=[pltpu.VMEM((2 * d, d), jnp.bfloat16),
                        pltpu.VMEM((M_HEADS, dv, dk), jnp.float32),
                        pltpu.VMEM((M_HEADS, BF16_ROWS, dk), jnp.float32),
                        pltpu.VMEM((M_HEADS, SUBLANES, LANES), jnp.float32)],
        compiler_params=_PARAMS,
        name="mlstm_mixer",
    )(x, g_pre, *[a for a, _ in per_mixer], g_post, *[w for w, _ in to_cast])
    return out[0], out[1:]


def _ffn_ple(x, p, seq, layer, *per_layer):
    n, d = x.shape
    w_down = per_layer[3][0]
    return pl.pallas_call(
        functools.partial(_ffn_ple_body, seq // TM_FFN),
        grid=(n // TM_FFN,),
        in_specs=([_tile_spec(TM_FFN, d), pl.BlockSpec((None, TM_FFN, p.shape[2]), lambda i: (layer, i, 0))]
                  + [_layer_spec(a, at) for a, at in per_layer]),
        out_specs=_tile_spec(TM_FFN, d),
        out_shape=jax.ShapeDtypeStruct(x.shape, x.dtype),
        scratch_shapes=[pltpu.VMEM((SUBLANES, w_down.shape[1]), jnp.float32)],
        compiler_params=_PARAMS,
        name="ffn_ple",
    )(x, p, *[a for a, _ in per_layer])


def kernel(x, p, norm_mix_pre, norm_mix_post, norm_ffn_pre, norm_ffn_post, sc_w_in, sc_w_conv, sc_w_out, ml_w_in, ml_gate_bias, ml_head_norm, ml_w_out, ffn_w_up, ffn_w_conv, ffn_w_down, ple_norm_gate, ple_w_gate, ple_w_proj, ple_norm_out):
    bsz, seq, d = x.shape
    depth = p.shape[0]
    assert all(seq % t == 0 and t % GROUP_ROWS == 0 for t in (TM_CONV, TM_MLSTM, TM_FFN))
    assert TM_MLSTM % MLSTM_CHUNK == 0
    bf = jnp.bfloat16
    qkvo = 3 * d
    n_gates = 2 * M_HEADS

    def rows(g):
        return g[:, None, :]

    w_g = ml_w_in[:, :, qkvo:]
    w_g_pad = jnp.pad(w_g, ((0, 0), (0, 0), (0, LANES - n_gates))).astype(bf)
    w_gt = jnp.swapaxes(w_g, 1, 2).astype(bf)
    bias = ml_gate_bias.astype(jnp.float32)
    b_row = rows(jnp.pad(bias, ((0, 0), (0, LANES - n_gates))))
    b_col = jnp.broadcast_to(bias[:, :, None], bias.shape + (LANES,))
    h_norm = jnp.broadcast_to(ml_head_norm[:, :, None], ml_head_norm.shape + (MLSTM_CHUNK,))
    mix_in, mix_out = sc_w_in[:1].astype(bf), sc_w_out[:1].astype(bf)
    ffn_f32 = (ffn_w_up, ffn_w_down, ple_w_gate, ple_w_proj)
    g_pre, g_post = rows(norm_mix_pre), rows(norm_mix_post)

    xf = x.reshape(bsz * seq, d)
    pf = p.reshape(depth, bsz * seq, p.shape[-1])
    for i in range(depth):
        j = i // N_MIXERS
        to_cast = [(w, i) for w in ffn_f32]
        if i % N_MIXERS == 0:
            to_cast += [(ml_w_in, j), (ml_w_out, j)]
            xf, cast = _conv_mixer(xf, seq, i, g_pre, (mix_in, 0), (sc_w_conv, j), (mix_out, 0), g_post, to_cast)
        else:
            if j + 1 < sc_w_in.shape[0]:
                to_cast += [(sc_w_in, j + 1), (sc_w_out, j + 1)]
            per_mixer = [(mix_in, 0), (w_g_pad, j), (w_gt, j), (b_row, j), (b_col, j), (h_norm, j), (mix_out, 0)]
            xf, cast = _mlstm_mixer(xf, seq, i, g_pre, per_mixer, g_post, to_cast)
        (w_up, w_down, w_gate, w_proj), (mix_in, mix_out) = cast[:4], (list(cast[4:]) + [None, None])[:2]
        xf = _ffn_ple(xf, pf, seq, i, (rows(norm_ffn_pre), i), (w_up, 0), (ffn_w_conv, i), (w_down, 0),
                      (rows(norm_ffn_post), i), (rows(ple_norm_gate), i), (w_gate, 0), (w_proj, 0),
                      (rows(ple_norm_out), i))
    return xf.reshape(bsz, seq, d)
```
